```python
import jax
import jax.numpy as jnp
from jax import lax
import numpy as np

D_MODEL = 2048
BATCH = 4
SEQ = 2048
DEPTH = 4
DEC_BATCH = 32
DEC_SEQ = 4
PAST_LEN = 16384
PAGE_SIZE = 128

N_MIXERS = 3
N_NSA = len(range(0, DEPTH, N_MIXERS))
N_SWA = len(range(1, DEPTH, N_MIXERS))
N_GLA = len(range(2, DEPTH, N_MIXERS))

NSA_HEADS = 16
NSA_KV = 2
NSA_HD = D_MODEL // NSA_HEADS
NSA_REP = NSA_HEADS // NSA_KV
NSA_BLOCK = 64
NSA_CMP_HID = NSA_HD
NSA_TOPN = 16
NSA_WINDOW = 512
NSA_SEL_QB = 64
NSA_IN = NSA_HEADS * NSA_HD + 3 * 2 * NSA_KV * NSA_HD + 3 * NSA_HEADS
SWA_HEADS = 32
SWA_KV = 4
SWA_HD = D_MODEL // SWA_HEADS
SWA_REP = SWA_HEADS // SWA_KV
SWA_WINDOW = 128
SWA_IN = SWA_HEADS * SWA_HD + 2 * SWA_KV * SWA_HD
GLA_HEADS = 4
GLA_DK = D_MODEL // 2 // GLA_HEADS
GLA_DV = D_MODEL // GLA_HEADS
GLA_RANK = 16
GLA_NORMALIZER = 16.0
GLA_CHUNK = 64
GLA_IN = 2 * GLA_HEADS * GLA_DK + 2 * GLA_HEADS * GLA_DV + GLA_RANK
N_EXPERTS = 32
TOP_K = 4
D_FF = D_MODEL
SWIGLU_LIMIT = 7.0
SWIGLU_ALPHA = 1.702
MOE_BLOCK = 128
QBLOCK = 128
NEG_INF = -1e30
EPS = 1e-6

kernel_name = 'hybrid_nsa_swa_gla_moe_adaln_step'


def _rmsnorm(x, g):
    xf = x.astype(jnp.float32)
    y = xf * lax.rsqrt(jnp.mean(xf * xf, axis=-1, keepdims=True) + EPS)
    return y.astype(x.dtype) * g


def _alibi_slopes(n_heads, n_groups):
    h = jnp.arange(1, n_heads + 1, dtype=jnp.float32)
    return jnp.exp2(-8.0 * h / n_heads).reshape(n_groups, n_heads // n_groups)


def _adaln(c, w, b):
    m = jax.nn.silu(c) @ w + b
    return m.reshape(c.shape[0], 6, 1, D_MODEL)


def _attend(q, k, v, mask, dist, slopes, sink=None):
    s = jnp.einsum('...qgrd,...kgd->...qgrk', q, k).astype(jnp.float32) * (q.shape[-1] ** -0.5)
    s = s - slopes[:, :, None] * dist[..., :, None, None, :]
    m = mask[..., :, None, None, :]
    s = jnp.where(m, s, NEG_INF)
    if sink is not None:
        sk = jnp.broadcast_to(sink.astype(jnp.float32)[:, :, None], s.shape[:-1] + (1,))
        p = jax.nn.softmax(jnp.concatenate([s, sk], axis=-1), axis=-1)[..., :-1]
    else:
        p = jax.nn.softmax(s, axis=-1)
    p = jnp.where(m, p, 0.0)
    o = jnp.einsum('...qgrk,...kgd->...qgrd', p.astype(v.dtype), v)
    return o, p


def _window_prompt(q, kv, window, slopes, sink=None):
    B, T = q.shape[:2]
    nqb = T // QBLOCK
    n_prev = -(-(window - 1) // QBLOCK)
    kb = (n_prev + 1) * QBLOCK
    pad = jnp.pad(kv, [(0, 0), (n_prev * QBLOCK, 0)] + [(0, 0)] * (kv.ndim - 2))
    blocks = pad.reshape((B, nqb + n_prev, QBLOCK) + kv.shape[2:])
    band = jnp.concatenate([blocks[:, j:j + nqb] for j in range(n_prev + 1)], axis=2)
    qpos = jnp.arange(T).reshape(nqb, QBLOCK)
    kpos = (jnp.arange(nqb)[:, None] - n_prev) * QBLOCK + jnp.arange(kb)[None, :]
    rel = qpos[:, :, None] - kpos[:, None, :]
    mask = (kpos[:, None, :] >= 0) & (rel >= 0) & (rel < window)
    qb = q.reshape((B, nqb, QBLOCK) + q.shape[2:])
    o, _ = _attend(qb, band[..., 0, :], band[..., 1, :], mask, rel.astype(jnp.float32), slopes, sink)
    return o.reshape(q.shape)


def _window_sample(q, kv_new, buf, window, slopes, sink=None):
    S = q.shape[1]
    wb = buf.shape[1]
    kv = jnp.concatenate([buf.astype(kv_new.dtype), kv_new], axis=1)
    qpos = PAST_LEN + jnp.arange(S)
    kpos = PAST_LEN - wb + jnp.arange(wb + S)
    rel = qpos[:, None] - kpos[None, :]
    mask = (rel >= 0) & (rel < window)
    o, _ = _attend(q, kv[..., 0, :], kv[..., 1, :], mask, rel.astype(jnp.float32), slopes, sink)
    return o, kv[:, S:]


def _sel_attend(q, k, v, kpos, qpos, slopes):
    s = jnp.einsum('bqgrd,bqgkd->bqgrk', q, k).astype(jnp.float32) * (q.shape[-1] ** -0.5)
    rel = qpos[None, :, None, None] - kpos
    s = s - slopes[None, None, :, :, None] * rel[:, :, :, None, :].astype(jnp.float32)
    m = (rel >= 0)[:, :, :, None, :]
    p = jnp.where(m, jax.nn.softmax(jnp.where(m, s, NEG_INF), axis=-1), 0.0)
    return jnp.einsum('bqgrk,bqgkd->bqgrd', p.astype(v.dtype), v)


def _nsa_project(h, w_in):
    B, T = h.shape[:2]
    p = h @ w_in
    nq = NSA_HEADS * NSA_HD
    nkv = 2 * NSA_KV * NSA_HD
    q = p[..., :nq].reshape(B, T, NSA_KV, NSA_REP, NSA_HD)
    kv = p[..., nq:nq + 3 * nkv].reshape(B, T, 3, NSA_KV, 2, NSA_HD)
    gates = jax.nn.sigmoid(p[..., nq + 3 * nkv:].reshape(B, T, 3, NSA_KV, NSA_REP))
    return q, kv[:, :, 0], kv[:, :, 1], kv[:, :, 2], gates


def _nsa_compress(blocks, pe, w1, w2):
    x = blocks + pe[:, None]
    hid = jax.nn.silu(jnp.einsum('...nlgcd,cldh->...ngch', x, w1))
    return jnp.einsum('...ngch,chd->...ngcd', hid, w2)


def _nsa_cmp_attend(q, cmp, qpos, slopes):
    nb = cmp.shape[-4]
    j = jnp.arange(nb)
    mask = j[None, :] < (qpos // NSA_BLOCK)[:, None]
    dist = (qpos[:, None] - (j[None, :] * NSA_BLOCK + NSA_BLOCK - 1)).astype(jnp.float32)
    o, p = _attend(q, cmp[..., 0, :], cmp[..., 1, :], mask, dist, slopes)
    return o, p.sum(axis=-2)


def _nsa_merge(gates, o_c, o_s, o_w, w_out):
    o = gates[:, :, 0, :, :, None] * o_c + gates[:, :, 1, :, :, None] * o_s + gates[:, :, 2, :, :, None] * o_w
    return o.reshape(o.shape[0], o.shape[1], -1) @ w_out


def _nsa_layer(hp, hs, cache_cmp, cache_slc, win_buf, page_table, li, w_in, pe, w1, w2, w_out, slopes):
    L = NSA_BLOCK
    B, T = hp.shape[:2]
    q, kvc, kvs, kvw, gates = _nsa_project(hp, w_in)
    nb = T // L
    qpos = jnp.arange(T)
    cmp = _nsa_compress(kvc.reshape(B, nb, L, NSA_KV, 2, NSA_HD), pe, w1, w2)
    o_c, imp = _nsa_cmp_attend(q, cmp, qpos, slopes)
    cur = (qpos // L)[:, None]
    j = jnp.arange(nb)[None, :]
    valid = j <= cur
    forced = valid & ((j == 0) | (j >= cur - 1))
    score = jnp.where(forced[:, None, :], jnp.inf, jnp.where(valid[:, None, :], imp, -jnp.inf))
    _, sel = lax.top_k(score, min(NSA_TOPN, nb))
    kvg = kvs.reshape(B, nb, L, NSA_KV, 2, NSA_HD).transpose(0, 3, 1, 2, 4, 5)
    b_idx = jnp.arange(B)[:, None, None, None]
    g_idx = jnp.arange(NSA_KV)[None, None, :, None]

    def sel_block(args):
        q_b, sel_b, qpos_b = args
        kv_b = kvg[b_idx, g_idx, sel_b]
        kv_b = kv_b.reshape(sel_b.shape[:3] + (-1, 2, NSA_HD))
        kpos = (sel_b[..., None] * L + jnp.arange(L)).reshape(sel_b.shape[:3] + (-1,))
        return _sel_attend(q_b, kv_b[..., 0, :], kv_b[..., 1, :], kpos, qpos_b, slopes)

    nqs = T // NSA_SEL_QB
    blk = lambda a: a.reshape((B, nqs, NSA_SEL_QB) + a.shape[2:]).swapaxes(0, 1)
    o_s = lax.map(sel_block, (blk(q), blk(sel), qpos.reshape(nqs, NSA_SEL_QB)))
    o_s = o_s.swapaxes(0, 1).reshape(q.shape)
    o_w = _window_prompt(q, kvw, NSA_WINDOW, slopes)
    yp = _nsa_merge(gates, o_c, o_s, o_w, w_out)
    win_p = kvw[:, T - min(NSA_WINDOW, T):]

    Bd, S = hs.shape[:2]
    q, kvc_s, kvs_s, kvw_s, gates = _nsa_project(hs, w_in)
    nbp = PAST_LEN // L
    qpos = PAST_LEN + jnp.arange(S)
    past = cache_cmp[li, page_table].reshape(Bd, nbp, L, NSA_KV, 2, NSA_HD).astype(hs.dtype)
    cmp = _nsa_compress(past, pe, w1, w2)
    o_c, imp = _nsa_cmp_attend(q, cmp, qpos, slopes)
    cur = (qpos // L)[:, None]
    j = jnp.arange(nbp)[None, :]
    valid = j < cur
    forced = valid & ((j == 0) | (j == cur - 1))
    score = jnp.where(forced[:, None, :], jnp.inf, jnp.where(valid[:, None, :], imp, -jnp.inf))
    _, sel = lax.top_k(score, min(NSA_TOPN - 1, nbp))
    bpp = PAGE_SIZE // L
    b_idx = jnp.arange(Bd)[:, None, None, None]
    phys = page_table[b_idx, sel // bpp]
    rows = (sel % bpp)[..., None] * L + jnp.arange(L)
    kv_past = cache_slc[li, phys[..., None], rows, jnp.arange(NSA_KV)[None, None, :, None, None]]
    kv_past = kv_past.reshape(sel.shape[:3] + (-1, 2, NSA_HD)).astype(kvs_s.dtype)
    kpos_past = (sel[..., None] * L + jnp.arange(L)).reshape(sel.shape[:3] + (-1,))
    kv_cur = jnp.broadcast_to(kvs_s.transpose(0, 2, 1, 3, 4)[:, None], (Bd, S, NSA_KV, S, 2, NSA_HD))
    kv_sel = jnp.concatenate([kv_past, kv_cur], axis=3)
    kpos = jnp.concatenate([kpos_past, jnp.broadcast_to(qpos, (Bd, S, NSA_KV, S))], axis=3)
    o_s = _sel_attend(q, kv_sel[..., 0, :], kv_sel[..., 1, :], kpos, qpos, slopes)
    o_w, win_s = _window_sample(q, kvw_s, win_buf, NSA_WINDOW, slopes)
    ys = _nsa_merge(gates, o_c, o_s, o_w, w_out)
    return yp, ys, (kvc, kvc_s, kvs, kvs_s, win_p, win_s)


def _swa_project(h, w_in, b_in):
    B, T = h.shape[:2]
    p = h @ w_in + b_in
    nq = SWA_HEADS * SWA_HD
    q = p[..., :nq].reshape(B, T, SWA_KV, SWA_REP, SWA_HD)
    kv = p[..., nq:].reshape(B, T, SWA_KV, 2, SWA_HD)
    return q, kv


def _swa_layer(hp, hs, buf, w_in, b_in, sinks, w_out, b_out, slopes):
    sink = sinks.reshape(SWA_KV, SWA_REP)
    B, T = hp.shape[:2]
    q, kv = _swa_project(hp, w_in, b_in)
    o = _window_prompt(q, kv, SWA_WINDOW, slopes, sink)
    yp = o.reshape(B, T, -1) @ w_out + b_out
    buf_p = kv[:, T - min(SWA_WINDOW, T):]
    Bd, S = hs.shape[:2]
    q, kv_s = _swa_project(hs, w_in, b_in)
    o, buf_s = _window_sample(q, kv_s, buf, SWA_WINDOW, slopes, sink)
    ys = o.reshape(Bd, S, -1) @ w_out + b_out
    return yp, ys, (buf_p, buf_s)


def _gla_project(h, w_in, w_a2, b_a):
    B, T = h.shape[:2]
    p = h @ w_in
    nk = GLA_HEADS * GLA_DK
    nv = GLA_HEADS * GLA_DV
    q = p[..., :nk].reshape(B, T, GLA_HEADS, GLA_DK).astype(jnp.float32) * (GLA_DK ** -0.5)
    k = p[..., nk:2 * nk].reshape(B, T, GLA_HEADS, GLA_DK).astype(jnp.float32)
    v = p[..., 2 * nk:2 * nk + nv].reshape(B, T, GLA_HEADS, GLA_DV).astype(jnp.float32)
    r = p[..., 2 * nk + nv:2 * nk + 2 * nv].reshape(B, T, GLA_HEADS, GLA_DV)
    a = (p[..., 2 * nk + 2 * nv:] @ w_a2 + b_a).astype(jnp.float32)
    g = (jax.nn.log_sigmoid(a) / GLA_NORMALIZER).reshape(B, T, GLA_HEADS, GLA_DK)
    return q, k, v, g, r


def _gla_chunk(state, q, k, v, g):
    C = q.shape[1]
    b = jnp.cumsum(g, axis=1)
    causal = jnp.tril(jnp.ones((C, C), dtype=bool))
    o_inter = jnp.einsum('bthk,bhkv->bthv', q * jnp.exp(b), state)
    diff = b[:, :, None] - b[:, None, :]
    decay = jnp.exp(jnp.where(causal[None, :, :, None, None], diff, -jnp.inf))
    a = jnp.einsum('bthk,btshk,bshk->btsh', q, decay, k)
    o_intra = jnp.einsum('btsh,bshv->bthv', a, v)
    b_last = b[:, -1]
    new_state = jnp.exp(b_last)[..., None] * state + jnp.einsum('bshk,bshv->bhkv', k * jnp.exp(b_last[:, None] - b), v)
    return new_state, o_inter + o_intra


def _gla_layer(hp, hs, state, w_in, w_a2, b_a, norm, w_out):
    def readout(o, r, h):
        y = _rmsnorm(o, norm).astype(h.dtype) * jax.nn.silu(r)
        return y.reshape(h.shape[0], h.shape[1], -1) @ w_out

    B, T = hp.shape[:2]
    q, k, v, g, r = _gla_project(hp, w_in, w_a2, b_a)
    nc = T // GLA_CHUNK
    chunks = lambda a: a.reshape((B, nc, GLA_CHUNK) + a.shape[2:]).swapaxes(0, 1)
    s0 = jnp.zeros((B, GLA_HEADS, GLA_DK, GLA_DV), jnp.float32)
    s_p, o = lax.scan(lambda s, xs: _gla_chunk(s, *xs), s0, (chunks(q), chunks(k), chunks(v), chunks(g)))
    o = o.swapaxes(0, 1).reshape(B, T, GLA_HEADS, GLA_DV)
    yp = readout(o, r, hp)
    q, k, v, g, r = _gla_project(hs, w_in, w_a2, b_a)
    s_s, o = _gla_chunk(state.astype(jnp.float32), q, k, v, g)
    ys = readout(o, r, hs)
    return yp, ys, (s_p.astype(hp.dtype), s_s.astype(state.dtype))


def _clamped_swiglu(gu):
    gate = jnp.minimum(gu[..., :D_FF], SWIGLU_LIMIT)
    up = jnp.clip(gu[..., D_FF:], -SWIGLU_LIMIT, SWIGLU_LIMIT)
    return gate * jax.nn.sigmoid(SWIGLU_ALPHA * gate) * (up + 1.0)


def _moe(x, w_r, b_r, w1, b1, w2, b2):
    N, D = x.shape
    logits = (x @ w_r + b_r).astype(jnp.float32)
    top_v, top_i = lax.top_k(logits, TOP_K)
    gate = jax.nn.softmax(top_v, axis=-1)
    A = N * TOP_K
    e_flat = top_i.reshape(-1)
    tok_flat = jnp.repeat(jnp.arange(N, dtype=jnp.int32), TOP_K)
    order = jnp.argsort(e_flat)
    e_sorted = e_flat[order]
    counts = jnp.bincount(e_flat, length=N_EXPERTS)
    padded = (counts + MOE_BLOCK - 1) // MOE_BLOCK * MOE_BLOCK
    start = jnp.cumsum(counts) - counts
    pend = jnp.cumsum(padded)
    pstart = pend - padded
    dest = pstart[e_sorted] + jnp.arange(A) - start[e_sorted]
    n_blk = -(-(A + N_EXPERTS * (MOE_BLOCK - 1)) // MOE_BLOCK)
    P = n_blk * MOE_BLOCK
    row_tok = jnp.full((P,), N, jnp.int32).at[dest].set(tok_flat[order])
    row_gate = jnp.zeros((P,), jnp.float32).at[dest].set(gate.reshape(-1)[order])
    blk_e = jnp.minimum(jnp.searchsorted(pend, jnp.arange(n_blk) * MOE_BLOCK, side='right'), N_EXPERTS - 1)
    x_pad = jnp.concatenate([x, jnp.zeros((1, D), x.dtype)], axis=0)

    def expert_block(args):
        tok, gt, e = args
        gu = x_pad[tok] @ w1[e] + b1[e]
        y = _clamped_swiglu(gu) @ w2[e] + b2[e]
        return y * gt[:, None].astype(y.dtype)

    ys = lax.map(expert_block, (row_tok.reshape(n_blk, MOE_BLOCK), row_gate.reshape(n_blk, MOE_BLOCK), blk_e))
    return jnp.zeros((N + 1, D), ys.dtype).at[row_tok].add(ys.reshape(P, D))[:N]


def setup_inputs(seed: int = 0) -> dict:
    key = jax.random.key(seed)
    ks = iter(jax.random.split(key, 64))
    D = D_MODEL
    n_pages = PAST_LEN // PAGE_SIZE
    n_used = DEC_BATCH * n_pages
    n_pool = n_used + max(1, n_used // 4)
    nsa_buf = min(NSA_WINDOW, PAST_LEN)
    swa_buf = min(SWA_WINDOW, PAST_LEN)

    def nrm(shape, scale=1.0):
        return jax.random.normal(next(ks), shape, jnp.float32) * scale

    return {
        'x_prompt': nrm((BATCH, SEQ, D)),
        'x_sample': nrm((DEC_BATCH, DEC_SEQ, D)),
        'cache_nsa_cmp': nrm((N_NSA, n_pool, PAGE_SIZE, NSA_KV, 2, NSA_HD)),
        'cache_nsa_slc': nrm((N_NSA, n_pool, PAGE_SIZE, NSA_KV, 2, NSA_HD)),
        'state_nsa_win': nrm((N_NSA, DEC_BATCH, nsa_buf, NSA_KV, 2, NSA_HD)),
        'state_swa_kv': nrm((N_SWA, DEC_BATCH, swa_buf, SWA_KV, 2, SWA_HD)),
        'state_gla': nrm((N_GLA, DEC_BATCH, GLA_HEADS, GLA_DK, GLA_DV), 0.5),
        'page_table': jax.random.permutation(next(ks), n_pool)[:n_used].reshape(DEC_BATCH, n_pages).astype(jnp.int32),
        'c_prompt': nrm((BATCH, D)),
        'c_sample': nrm((DEC_BATCH, D)),
        'ada_w': nrm((DEPTH, D, 6 * D), 0.5 * D ** -0.5),
        'ada_b': nrm((DEPTH, 6 * D), 0.02),
        'norm_mix': 1.0 + nrm((DEPTH, D), 0.05),
        'norm_ffn': 1.0 + nrm((DEPTH, D), 0.05),
        'norm_final': 1.0 + nrm((D,), 0.05),
        'nsa_w_in': nrm((N_NSA, D, NSA_IN), D ** -0.5),
        'nsa_cmp_pe': nrm((N_NSA, NSA_BLOCK, 2, NSA_HD), 0.1),
        'nsa_cmp_w1': nrm((N_NSA, 2, NSA_BLOCK, NSA_HD, NSA_CMP_HID), (NSA_BLOCK * NSA_HD) ** -0.5),
        'nsa_cmp_w2': nrm((N_NSA, 2, NSA_CMP_HID, NSA_HD), NSA_CMP_HID ** -0.5),
        'nsa_w_out': nrm((N_NSA, NSA_HEADS * NSA_HD, D), (NSA_HEADS * NSA_HD) ** -0.5),
        'swa_w_in': nrm((N_SWA, D, SWA_IN), D ** -0.5),
        'swa_b_in': nrm((N_SWA, SWA_IN), 0.02),
        'swa_sinks': nrm((N_SWA, SWA_HEADS), 0.5),
        'swa_w_out': nrm((N_SWA, SWA_HEADS * SWA_HD, D), (SWA_HEADS * SWA_HD) ** -0.5),
        'swa_b_out': nrm((N_SWA, D), 0.02),
        'gla_w_in': nrm((N_GLA, D, GLA_IN), D ** -0.5),
        'gla_w_a2': nrm((N_GLA, GLA_RANK, GLA_HEADS * GLA_DK), GLA_RANK ** -0.5),
        'gla_b_a': nrm((N_GLA, GLA_HEADS * GLA_DK), 0.1),
        'gla_norm': 1.0 + nrm((N_GLA, GLA_DV), 0.05),
        'gla_w_out': nrm((N_GLA, GLA_HEADS * GLA_DV, D), (GLA_HEADS * GLA_DV) ** -0.5),
        'moe_w_router': nrm((DEPTH, D, N_EXPERTS), D ** -0.5),
        'moe_b_router': nrm((DEPTH, N_EXPERTS), 0.01),
        'moe_w1': nrm((DEPTH, N_EXPERTS, D, 2 * D_FF), D ** -0.5),
        'moe_b1': nrm((DEPTH, N_EXPERTS, 2 * D_FF), 0.02),
        'moe_w2': nrm((DEPTH, N_EXPERTS, D_FF, D), D_FF ** -0.5),
        'moe_b2': nrm((DEPTH, N_EXPERTS, D), 0.02),
    }


def reference(x_prompt, x_sample, cache_nsa_cmp, cache_nsa_slc, state_nsa_win, state_swa_kv, state_gla,
              page_table, c_prompt, c_sample, ada_w, ada_b, norm_mix, norm_ffn, norm_final,
              nsa_w_in, nsa_cmp_pe, nsa_cmp_w1, nsa_cmp_w2, nsa_w_out,
              swa_w_in, swa_b_in, swa_sinks, swa_w_out, swa_b_out,
              gla_w_in, gla_w_a2, gla_b_a, gla_norm, gla_w_out,
              moe_w_router, moe_b_router, moe_w1, moe_b1, moe_w2, moe_b2):
    B, T = x_prompt.shape[:2]
    Bd, S = x_sample.shape[:2]
    n_p = B * T
    nsa_slopes = _alibi_slopes(NSA_HEADS, NSA_KV)
    swa_slopes = _alibi_slopes(SWA_HEADS, SWA_KV)
    xp, xs = x_prompt, x_sample
    nsa_new, swa_new, gla_new = [], [], []
    for i in range(DEPTH):
        mp = _adaln(c_prompt, ada_w[i], ada_b[i])
        ms = _adaln(c_sample, ada_w[i], ada_b[i])
        hp = _rmsnorm(xp, norm_mix[i]) * (1.0 + mp[:, 1]) + mp[:, 0]
        hs = _rmsnorm(xs, norm_mix[i]) * (1.0 + ms[:, 1]) + ms[:, 0]
        kind, li = i % N_MIXERS, i // N_MIXERS
        if kind == 0:
            yp, ys, st = _nsa_layer(hp, hs, cache_nsa_cmp, cache_nsa_slc, state_nsa_win[li], page_table, li,
                                    nsa_w_in[li], nsa_cmp_pe[li], nsa_cmp_w1[li], nsa_cmp_w2[li], nsa_w_out[li],
                                    nsa_slopes)
            nsa_new.append(st)
        elif kind == 1:
            yp, ys, st = _swa_layer(hp, hs, state_swa_kv[li], swa_w_in[li], swa_b_in[li], swa_sinks[li],
                                    swa_w_out[li], swa_b_out[li], swa_slopes)
            swa_new.append(st)
        else:
            yp, ys, st = _gla_layer(hp, hs, state_gla[li], gla_w_in[li], gla_w_a2[li], gla_b_a[li],
                                    gla_norm[li], gla_w_out[li])
            gla_new.append(st)
        xp = xp + mp[:, 2] * yp
        xs = xs + ms[:, 2] * ys
        hp = _rmsnorm(xp, norm_ffn[i]) * (1.0 + mp[:, 4]) + mp[:, 3]
        hs = _rmsnorm(xs, norm_ffn[i]) * (1.0 + ms[:, 4]) + ms[:, 3]
        h_all = jnp.concatenate([hp.reshape(n_p, D_MODEL), hs.reshape(Bd * S, D_MODEL)], axis=0)
        y_all = _moe(h_all, moe_w_router[i], moe_b_router[i], moe_w1[i], moe_b1[i], moe_w2[i], moe_b2[i])
        xp = xp + mp[:, 5] * y_all[:n_p].reshape(B, T, D_MODEL)
        xs = xs + ms[:, 5] * y_all[n_p:].reshape(Bd, S, D_MODEL)
    y_prompt = _rmsnorm(xp, norm_final)
    y_sample = _rmsnorm(xs, norm_final)
    nsa_cmp_prompt = jnp.stack([st[0] for st in nsa_new])
    nsa_cmp_sample = jnp.stack([st[1] for st in nsa_new])
    nsa_slc_prompt = jnp.stack([st[2] for st in nsa_new])
    nsa_slc_sample = jnp.stack([st[3] for st in nsa_new])
    nsa_win_prompt = jnp.stack([st[4] for st in nsa_new])
    nsa_win_sample = jnp.stack([st[5] for st in nsa_new])
    swa_kv_prompt = jnp.stack([st[0] for st in swa_new])
    swa_kv_sample = jnp.stack([st[1] for st in swa_new])
    gla_state_prompt = jnp.stack([st[0] for st in gla_new])
    gla_state_sample = jnp.stack([st[1] for st in gla_new])
    return (y_prompt, y_sample, nsa_cmp_prompt, nsa_cmp_sample, nsa_slc_prompt, nsa_slc_sample,
            nsa_win_prompt, nsa_win_sample, swa_kv_prompt, swa_kv_sample, gla_state_prompt, gla_state_sample)
```

```python
import functools

import jax
import jax.numpy as jnp
from jax import lax
from jax.experimental import pallas as pl
from jax.experimental.pallas import tpu as pltpu

D_MODEL = 2048
DEPTH = 4
PAST_LEN = 16384
PAGE_SIZE = 128
N_MIXERS = 3

NSA_HEADS = 16
NSA_KV = 2
NSA_HD = D_MODEL // NSA_HEADS
NSA_REP = NSA_HEADS // NSA_KV
NSA_BLOCK = 64
NSA_TOPN = 16
NSA_WINDOW = 512
NSA_SEL_QB = 64
SWA_HEADS = 32
SWA_KV = 4
SWA_HD = D_MODEL // SWA_HEADS
SWA_REP = SWA_HEADS // SWA_KV
SWA_WINDOW = 128
GLA_HEADS = 4
GLA_DK = D_MODEL // 2 // GLA_HEADS
GLA_DV = D_MODEL // GLA_HEADS
GLA_NORMALIZER = 16.0
GLA_CHUNK = 64
N_EXPERTS = 32
TOP_K = 4
D_FF = D_MODEL
SWIGLU_LIMIT = 7.0
SWIGLU_ALPHA = 1.702
QBLOCK = 128
NEG_INF = -1e30
EPS = 1e-6

VMEM_LIMIT_BYTES = 56 * 1024 * 1024
MOE_TM = 256
MOE_TF = 1024
MOE_TN = 1024
CAST_ROWS = 256


def _cast_rows(src_ref, dst_ref):
    n = dst_ref.shape[0] // CAST_ROWS

    def body(c, carry):
        r = pl.multiple_of(c * CAST_ROWS, CAST_ROWS)
        dst_ref[pl.ds(r, CAST_ROWS), :] = src_ref[pl.ds(r, CAST_ROWS), :].astype(jnp.bfloat16)
        return carry

    lax.fori_loop(0, n, body, 0)


def _moe_up_kernel(te_ref, tf_ref, tv_ref, x_ref, wg_ref, wu_ref, bg_ref, bu_ref, h_ref, wg_s, wu_s):
    i = pl.program_id(1)

    @pl.when(tf_ref[i] == 1)
    def _():
        _cast_rows(wg_ref, wg_s)
        _cast_rows(wu_ref, wu_s)

    @pl.when(tv_ref[i] == 1)
    def _():
        x = x_ref[...]
        g = jnp.dot(x, wg_s[...], preferred_element_type=jnp.float32) + bg_ref[...]
        u = jnp.dot(x, wu_s[...], preferred_element_type=jnp.float32) + bu_ref[...]
        g = jnp.minimum(g, SWIGLU_LIMIT)
        u = jnp.clip(u, -SWIGLU_LIMIT, SWIGLU_LIMIT)
        sig = 1.0 / (1.0 + jnp.exp(-SWIGLU_ALPHA * g))
        h_ref[...] = (g * sig * (u + 1.0)).astype(h_ref.dtype)

    @pl.when(tv_ref[i] == 0)
    def _():
        h_ref[...] = jnp.zeros_like(h_ref)


def _moe_down_kernel(te_ref, tf_ref, tv_ref, h_ref, w_ref, b_ref, gt_ref, y_ref, w_s):
    i = pl.program_id(1)

    @pl.when(tf_ref[i] == 1)
    def _():
        _cast_rows(w_ref, w_s)

    @pl.when(tv_ref[i] == 1)
    def _():
        y = jnp.dot(h_ref[...], w_s[...], preferred_element_type=jnp.float32) + b_ref[...]
        y_ref[...] = y * gt_ref[...]

    @pl.when(tv_ref[i] == 0)
    def _():
        y_ref[...] = jnp.zeros_like(y_ref)


def _moe_ffn(x_sorted, row_gate, tile_e, tile_first, tile_valid, layer, w1, b1, w2, b2):
    P, D = x_sorted.shape
    n_tiles = P // MOE_TM
    nj1 = D_FF // MOE_TF
    b1r = b1.reshape(DEPTH, N_EXPERTS, 1, 2 * D_FF)
    b2r = b2.reshape(DEPTH, N_EXPERTS, 1, D)
    params = pltpu.CompilerParams(dimension_semantics=("arbitrary", "arbitrary"),
                                  vmem_limit_bytes=VMEM_LIMIT_BYTES)
    h = pl.pallas_call(
        _moe_up_kernel,
        grid_spec=pltpu.PrefetchScalarGridSpec(
            num_scalar_prefetch=3,
            grid=(nj1, n_tiles),
            in_specs=[
                pl.BlockSpec((MOE_TM, D), lambda j, i, te, tf, tv: (i, 0)),
                pl.BlockSpec((None, None, D, MOE_TF), lambda j, i, te, tf, tv: (layer, te[i], 0, j)),
                pl.BlockSpec((None, None, D, MOE_TF), lambda j, i, te, tf, tv: (layer, te[i], 0, nj1 + j)),
                pl.BlockSpec((None, None, 1, MOE_TF), lambda j, i, te, tf, tv: (layer, te[i], 0, j)),
                pl.BlockSpec((None, None, 1, MOE_TF), lambda j, i, te, tf, tv: (layer, te[i], 0, nj1 + j)),
            ],
            out_specs=pl.BlockSpec((MOE_TM, MOE_TF), lambda j, i, te, tf, tv: (i, j)),
            scratch_shapes=[pltpu.VMEM((D, MOE_TF), jnp.bfloat16), pltpu.VMEM((D, MOE_TF), jnp.bfloat16)],
        ),
        out_shape=jax.ShapeDtypeStruct((P, D_FF), jnp.bfloat16),
        compiler_params=params,
        name="moe_up",
    )(tile_e, tile_first, tile_valid, x_sorted, w1, w1, b1r, b1r)
    y = pl.pallas_call(
        _moe_down_kernel,
        grid_spec=pltpu.PrefetchScalarGridSpec(
            num_scalar_prefetch=3,
            grid=(D // MOE_TN, n_tiles),
            in_specs=[
                pl.BlockSpec((MOE_TM, D_FF), lambda j, i, te, tf, tv: (i, 0)),
                pl.BlockSpec((None, None, D_FF, MOE_TN), lambda j, i, te, tf, tv: (layer, te[i], 0, j)),
                pl.BlockSpec((None, None, 1, MOE_TN), lambda j, i, te, tf, tv: (layer, te[i], 0, j)),
                pl.BlockSpec((MOE_TM, 1), lambda j, i, te, tf, tv: (i, 0)),
            ],
            out_specs=pl.BlockSpec((MOE_TM, MOE_TN), lambda j, i, te, tf, tv: (i, j)),
            scratch_shapes=[pltpu.VMEM((D_FF, MOE_TN), jnp.bfloat16)],
        ),
        out_shape=jax.ShapeDtypeStruct((P, D), jnp.float32),
        compiler_params=params,
        name="moe_down",
    )(tile_e, tile_first, tile_valid, h, w2, b2r, row_gate)
    return y


def _moe(x, w_r, b_r, layer, w1, b1, w2, b2):
    N, D = x.shape
    logits = (x @ w_r + b_r).astype(jnp.float32)
    top_v, top_i = lax.top_k(logits, TOP_K)
    gate = jax.nn.softmax(top_v, axis=-1)
    A = N * TOP_K
    e_flat = top_i.reshape(-1).astype(jnp.int32)
    onehot = (e_flat[:, None] == jnp.arange(N_EXPERTS, dtype=jnp.int32)[None, :]).astype(jnp.int32)
    csum = jnp.cumsum(onehot, axis=0)
    counts = csum[-1]
    rank = jnp.take_along_axis(csum, e_flat[:, None], axis=1)[:, 0] - 1
    padded = (counts + MOE_TM - 1) // MOE_TM * MOE_TM
    pend = jnp.cumsum(padded)
    pstart = pend - padded
    dest = pstart[e_flat] + rank
    n_tiles = -(-(A + N_EXPERTS * (MOE_TM - 1)) // MOE_TM)
    P = n_tiles * MOE_TM
    tile_start = jnp.arange(n_tiles, dtype=jnp.int32) * MOE_TM
    tile_valid = tile_start < pend[-1]
    tile_e = jnp.minimum(jnp.searchsorted(pend, tile_start, side='right'), N_EXPERTS - 1).astype(jnp.int32)
    prev_e = jnp.concatenate([jnp.full((1,), -1, jnp.int32), tile_e[:-1]])
    tile_first = ((tile_e != prev_e) & tile_valid).astype(jnp.int32)
    tok_flat = jnp.repeat(jnp.arange(N, dtype=jnp.int32), TOP_K)
    row_tok = jnp.full((P,), N, jnp.int32).at[dest].set(tok_flat)
    row_gate = jnp.zeros((P,), jnp.float32).at[dest].set(gate.reshape(-1))
    x_pad = jnp.concatenate([x.astype(jnp.bfloat16), jnp.zeros((1, D), jnp.bfloat16)], axis=0)
    x_sorted = x_pad[row_tok]
    y = _moe_ffn(x_sorted, row_gate[:, None], tile_e, tile_first, tile_valid.astype(jnp.int32), layer,
                 w1, b1, w2, b2)
    return y[dest.reshape(N, TOP_K)].sum(axis=1)


def _rmsnorm(x, g):
    xf = x.astype(jnp.float32)
    y = xf * lax.rsqrt(jnp.mean(xf * xf, axis=-1, keepdims=True) + EPS)
    return y.astype(x.dtype) * g


def _alibi_slopes(n_heads, n_groups):
    h = jnp.arange(1, n_heads + 1, dtype=jnp.float32)
    return jnp.exp2(-8.0 * h / n_heads).reshape(n_groups, n_heads // n_groups)


def _adaln(c, w, b):
    m = jax.nn.silu(c) @ w + b
    return m.reshape(c.shape[0], 6, 1, D_MODEL)


def _attend(q, k, v, mask, dist, slopes, sink=None):
    s = jnp.einsum('...qgrd,...kgd->...qgrk', q, k).astype(jnp.float32) * (q.shape[-1] ** -0.5)
    s = s - slopes[:, :, None] * dist[..., :, None, None, :]
    m = mask[..., :, None, None, :]
    s = jnp.where(m, s, NEG_INF)
    if sink is not None:
        sk = jnp.broadcast_to(sink.astype(jnp.float32)[:, :, None], s.shape[:-1] + (1,))
        p = jax.nn.softmax(jnp.concatenate([s, sk], axis=-1), axis=-1)[..., :-1]
    else:
        p = jax.nn.softmax(s, axis=-1)
    p = jnp.where(m, p, 0.0)
    o = jnp.einsum('...qgrk,...kgd->...qgrd', p.astype(v.dtype), v)
    return o, p


def _window_prompt(q, kv, window, slopes, sink=None):
    B, T = q.shape[:2]
    nqb = T // QBLOCK
    n_prev = -(-(window - 1) // QBLOCK)
    kb = (n_prev + 1) * QBLOCK
    pad = jnp.pad(kv, [(0, 0), (n_prev * QBLOCK, 0)] + [(0, 0)] * (kv.ndim - 2))
    blocks = pad.reshape((B, nqb + n_prev, QBLOCK) + kv.shape[2:])
    band = jnp.concatenate([blocks[:, j:j + nqb] for j in range(n_prev + 1)], axis=2)
    qpos = jnp.arange(T).reshape(nqb, QBLOCK)
    kpos = (jnp.arange(nqb)[:, None] - n_prev) * QBLOCK + jnp.arange(kb)[None, :]
    rel = qpos[:, :, None] - kpos[:, None, :]
    mask = (kpos[:, None, :] >= 0) & (rel >= 0) & (rel < window)
    qb = q.reshape((B, nqb, QBLOCK) + q.shape[2:])
    o, _ = _attend(qb, band[..., 0, :], band[..., 1, :], mask, rel.astype(jnp.float32), slopes, sink)
    return o.reshape(q.shape)


def _window_sample(q, kv_new, buf, window, slopes, sink=None):
    S = q.shape[1]
    wb = buf.shape[1]
    kv = jnp.concatenate([buf.astype(kv_new.dtype), kv_new], axis=1)
    qpos = PAST_LEN + jnp.arange(S)
    kpos = PAST_LEN - wb + jnp.arange(wb + S)
    rel = qpos[:, None] - kpos[None, :]
    mask = (rel >= 0) & (rel < window)
    o, _ = _attend(q, kv[..., 0, :], kv[..., 1, :], mask, rel.astype(jnp.float32), slopes, sink)
    return o, kv[:, S:]


def _sel_attend(q, k, v, kpos, qpos, slopes):
    s = jnp.einsum('bqgrd,bqgkd->bqgrk', q, k).astype(jnp.float32) * (q.shape[-1] ** -0.5)
    rel = qpos[None, :, None, None] - kpos
    s = s - slopes[None, None, :, :, None] * rel[:, :, :, None, :].astype(jnp.float32)
    m = (rel >= 0)[:, :, :, None, :]
    p = jnp.where(m, jax.nn.softmax(jnp.where(m, s, NEG_INF), axis=-1), 0.0)
    return jnp.einsum('bqgrk,bqgkd->bqgrd', p.astype(v.dtype), v)


def _nsa_project(h, w_in):
    B, T = h.shape[:2]
    p = h @ w_in
    nq = NSA_HEADS * NSA_HD
    nkv = 2 * NSA_KV * NSA_HD
    q = p[..., :nq].reshape(B, T, NSA_KV, NSA_REP, NSA_HD)
    kv = p[..., nq:nq + 3 * nkv].reshape(B, T, 3, NSA_KV, 2, NSA_HD)
    gates = jax.nn.sigmoid(p[..., nq + 3 * nkv:].reshape(B, T, 3, NSA_KV, NSA_REP))
    return q, kv[:, :, 0], kv[:, :, 1], kv[:, :, 2], gates


def _nsa_compress(blocks, pe, w1, w2):
    x = blocks + pe[:, None]
    hid = jax.nn.silu(jnp.einsum('...nlgcd,cldh->...ngch', x, w1))
    return jnp.einsum('...ngch,chd->...ngcd', hid, w2)


def _nsa_cmp_attend(q, cmp, qpos, slopes):
    nb = cmp.shape[-4]
    j = jnp.arange(nb)
    mask = j[None, :] < (qpos // NSA_BLOCK)[:, None]
    dist = (qpos[:, None] - (j[None, :] * NSA_BLOCK + NSA_BLOCK - 1)).astype(jnp.float32)
    o, p = _attend(q, cmp[..., 0, :], cmp[..., 1, :], mask, dist, slopes)
    return o, p.sum(axis=-2)


def _nsa_merge(gates, o_c, o_s, o_w, w_out):
    o = gates[:, :, 0, :, :, None] * o_c + gates[:, :, 1, :, :, None] * o_s + gates[:, :, 2, :, :, None] * o_w
    return o.reshape(o.shape[0], o.shape[1], -1) @ w_out


def _nsa_layer(hp, hs, cache_cmp, cache_slc, win_buf, page_table, li, w_in, pe, w1, w2, w_out, slopes):
    L = NSA_BLOCK
    B, T = hp.shape[:2]
    q, kvc, kvs, kvw, gates = _nsa_project(hp, w_in)
    nb = T // L
    qpos = jnp.arange(T)
    cmp = _nsa_compress(kvc.reshape(B, nb, L, NSA_KV, 2, NSA_HD), pe, w1, w2)
    o_c, imp = _nsa_cmp_attend(q, cmp, qpos, slopes)
    cur = (qpos // L)[:, None]
    j = jnp.arange(nb)[None, :]
    valid = j <= cur
    forced = valid & ((j == 0) | (j >= cur - 1))
    score = jnp.where(forced[:, None, :], jnp.inf, jnp.where(valid[:, None, :], imp, -jnp.inf))
    _, sel = lax.top_k(score, min(NSA_TOPN, nb))
    kvg = kvs.reshape(B, nb, L, NSA_KV, 2, NSA_HD).transpose(0, 3, 1, 2, 4, 5)
    b_idx = jnp.arange(B)[:, None, None, None]
    g_idx = jnp.arange(NSA_KV)[None, None, :, None]

    def sel_block(args):
        q_b, sel_b, qpos_b = args
        kv_b = kvg[b_idx, g_idx, sel_b]
        kv_b = kv_b.reshape(sel_b.shape[:3] + (-1, 2, NSA_HD))
        kpos = (sel_b[..., None] * L + jnp.arange(L)).reshape(sel_b.shape[:3] + (-1,))
        return _sel_attend(q_b, kv_b[..., 0, :], kv_b[..., 1, :], kpos, qpos_b, slopes)

    nqs = T // NSA_SEL_QB
    blk = lambda a: a.reshape((B, nqs, NSA_SEL_QB) + a.shape[2:]).swapaxes(0, 1)
    o_s = lax.map(sel_block, (blk(q), blk(sel), qpos.reshape(nqs, NSA_SEL_QB)))
    o_s = o_s.swapaxes(0, 1).reshape(q.shape)
    o_w = _window_prompt(q, kvw, NSA_WINDOW, slopes)
    yp = _nsa_merge(gates, o_c, o_s, o_w, w_out)
    win_p = kvw[:, T - min(NSA_WINDOW, T):]

    Bd, S = hs.shape[:2]
    q, kvc_s, kvs_s, kvw_s, gates = _nsa_project(hs, w_in)
    nbp = PAST_LEN // L
    qpos = PAST_LEN + jnp.arange(S)
    past = cache_cmp[li, page_table].reshape(Bd, nbp, L, NSA_KV, 2, NSA_HD).astype(hs.dtype)
    cmp = _nsa_compress(past, pe, w1, w2)
    o_c, imp = _nsa_cmp_attend(q, cmp, qpos, slopes)
    cur = (qpos // L)[:, None]
    j = jnp.arange(nbp)[None, :]
    valid = j < cur
    forced = valid & ((j == 0) | (j == cur - 1))
    score = jnp.where(forced[:, None, :], jnp.inf, jnp.where(valid[:, None, :], imp, -jnp.inf))
    _, sel = lax.top_k(score, min(NSA_TOPN - 1, nbp))
    bpp = PAGE_SIZE // L
    b_idx = jnp.arange(Bd)[:, None, None, None]
    phys = page_table[b_idx, sel // bpp]
    rows = (sel % bpp)[..., None] * L + jnp.arange(L)
    kv_past = cache_slc[li, phys[..., None], rows, jnp.arange(NSA_KV)[None, None, :, None, None]]
    kv_past = kv_past.reshape(sel.shape[:3] + (-1, 2, NSA_HD)).astype(kvs_s.dtype)
    kpos_past = (sel[..., None] * L + jnp.arange(L)).reshape(sel.shape[:3] + (-1,))
    kv_cur = jnp.broadcast_to(kvs_s.transpose(0, 2, 1, 3, 4)[:, None], (Bd, S, NSA_KV, S, 2, NSA_HD))
    kv_sel = jnp.concatenate([kv_past, kv_cur], axis=3)
    kpos = jnp.concatenate([kpos_past, jnp.broadcast_to(qpos, (Bd, S, NSA_KV, S))], axis=3)
    o_s = _sel_attend(q, kv_sel[..., 0, :], kv_sel[..., 1, :], kpos, qpos, slopes)
    o_w, win_s = _window_sample(q, kvw_s, win_buf, NSA_WINDOW, slopes)
    ys = _nsa_merge(gates, o_c, o_s, o_w, w_out)
    return yp, ys, (kvc, kvc_s, kvs, kvs_s, win_p, win_s)


def _swa_project(h, w_in, b_in):
    B, T = h.shape[:2]
    p = h @ w_in + b_in
    nq = SWA_HEADS * SWA_HD
    q = p[..., :nq].reshape(B, T, SWA_KV, SWA_REP, SWA_HD)
    kv = p[..., nq:].reshape(B, T, SWA_KV, 2, SWA_HD)
    return q, kv


def _swa_layer(hp, hs, buf, w_in, b_in, sinks, w_out, b_out, slopes):
    sink = sinks.reshape(SWA_KV, SWA_REP)
    B, T = hp.shape[:2]
    q, kv = _swa_project(hp, w_in, b_in)
    o = _window_prompt(q, kv, SWA_WINDOW, slopes, sink)
    yp = o.reshape(B, T, -1) @ w_out + b_out
    buf_p = kv[:, T - min(SWA_WINDOW, T):]
    Bd, S = hs.shape[:2]
    q, kv_s = _swa_project(hs, w_in, b_in)
    o, buf_s = _window_sample(q, kv_s, buf, SWA_WINDOW, slopes, sink)
    ys = o.reshape(Bd, S, -1) @ w_out + b_out
    return yp, ys, (buf_p, buf_s)


def _gla_project(h, w_in, w_a2, b_a):
    B, T = h.shape[:2]
    p = h @ w_in
    nk = GLA_HEADS * GLA_DK
    nv = GLA_HEADS * GLA_DV
    q = p[..., :nk].reshape(B, T, GLA_HEADS, GLA_DK).astype(jnp.float32) * (GLA_DK ** -0.5)
    k = p[..., nk:2 * nk].reshape(B, T, GLA_HEADS, GLA_DK).astype(jnp.float32)
    v = p[..., 2 * nk:2 * nk + nv].reshape(B, T, GLA_HEADS, GLA_DV).astype(jnp.float32)
    r = p[..., 2 * nk + nv:2 * nk + 2 * nv].reshape(B, T, GLA_HEADS, GLA_DV)
    a = (p[..., 2 * nk + 2 * nv:] @ w_a2 + b_a).astype(jnp.float32)
    g = (jax.nn.log_sigmoid(a) / GLA_NORMALIZER).reshape(B, T, GLA_HEADS, GLA_DK)
    return q, k, v, g, r


def _gla_chunk(state, q, k, v, g):
    C = q.shape[1]
    b = jnp.cumsum(g, axis=1)
    causal = jnp.tril(jnp.ones((C, C), dtype=bool))
    o_inter = jnp.einsum('bthk,bhkv->bthv', q * jnp.exp(b), state)
    diff = b[:, :, None] - b[:, None, :]
    decay = jnp.exp(jnp.where(causal[None, :, :, None, None], diff, -jnp.inf))
    a = jnp.einsum('bthk,btshk,bshk->btsh', q, decay, k)
    o_intra = jnp.einsum('btsh,bshv->bthv', a, v)
    b_last = b[:, -1]
    new_state = jnp.exp(b_last)[..., None] * state + jnp.einsum('bshk,bshv->bhkv', k * jnp.exp(b_last[:, None] - b), v)
    return new_state, o_inter + o_intra


def _gla_layer(hp, hs, state, w_in, w_a2, b_a, norm, w_out):
    def readout(o, r, h):
        y = _rmsnorm(o, norm).astype(h.dtype) * jax.nn.silu(r)
        return y.reshape(h.shape[0], h.shape[1], -1) @ w_out

    B, T = hp.shape[:2]
    q, k, v, g, r = _gla_project(hp, w_in, w_a2, b_a)
    nc = T // GLA_CHUNK
    chunks = lambda a: a.reshape((B, nc, GLA_CHUNK) + a.shape[2:]).swapaxes(0, 1)
    s0 = jnp.zeros((B, GLA_HEADS, GLA_DK, GLA_DV), jnp.float32)
    s_p, o = lax.scan(lambda s, xs: _gla_chunk(s, *xs), s0, (chunks(q), chunks(k), chunks(v), chunks(g)))
    o = o.swapaxes(0, 1).reshape(B, T, GLA_HEADS, GLA_DV)
    yp = readout(o, r, hp)
    q, k, v, g, r = _gla_project(hs, w_in, w_a2, b_a)
    s_s, o = _gla_chunk(state.astype(jnp.float32), q, k, v, g)
    ys = readout(o, r, hs)
    return yp, ys, (s_p.astype(hp.dtype), s_s.astype(state.dtype))


def kernel(x_prompt, x_sample, cache_nsa_cmp, cache_nsa_slc, state_nsa_win, state_swa_kv, state_gla,
           page_table, c_prompt, c_sample, ada_w, ada_b, norm_mix, norm_ffn, norm_final,
           nsa_w_in, nsa_cmp_pe, nsa_cmp_w1, nsa_cmp_w2, nsa_w_out,
           swa_w_in, swa_b_in, swa_sinks, swa_w_out, swa_b_out,
           gla_w_in, gla_w_a2, gla_b_a, gla_norm, gla_w_out,
           moe_w_router, moe_b_router, moe_w1, moe_b1, moe_w2, moe_b2):
    B, T = x_prompt.shape[:2]
    Bd, S = x_sample.shape[:2]
    n_p = B * T
    nsa_slopes = _alibi_slopes(NSA_HEADS, NSA_KV)
    swa_slopes = _alibi_slopes(SWA_HEADS, SWA_KV)
    xp, xs = x_prompt, x_sample
    nsa_new, swa_new, gla_new = [], [], []
    for i in range(DEPTH):
        mp = _adaln(c_prompt, ada_w[i], ada_b[i])
        ms = _adaln(c_sample, ada_w[i], ada_b[i])
        hp = _rmsnorm(xp, norm_mix[i]) * (1.0 + mp[:, 1]) + mp[:, 0]
        hs = _rmsnorm(xs, norm_mix[i]) * (1.0 + ms[:, 1]) + ms[:, 0]
        kind, li = i % N_MIXERS, i // N_MIXERS
        if kind == 0:
            yp, ys, st = _nsa_layer(hp, hs, cache_nsa_cmp, cache_nsa_slc, state_nsa_win[li], page_table, li,
                                    nsa_w_in[li], nsa_cmp_pe[li], nsa_cmp_w1[li], nsa_cmp_w2[li], nsa_w_out[li],
                                    nsa_slopes)
            nsa_new.append(st)
        elif kind == 1:
            yp, ys, st = _swa_layer(hp, hs, state_swa_kv[li], swa_w_in[li], swa_b_in[li], swa_sinks[li],
                                    swa_w_out[li], swa_b_out[li], swa_slopes)
            swa_new.append(st)
        else:
            yp, ys, st = _gla_layer(hp, hs, state_gla[li], gla_w_in[li], gla_w_a2[li], gla_b_a[li],
                                    gla_norm[li], gla_w_out[li])
            gla_new.append(st)
        xp = xp + mp[:, 2] * yp
        xs = xs + ms[:, 2] * ys
        hp = _rmsnorm(xp, norm_ffn[i]) * (1.0 + mp[:, 4]) + mp[:, 3]
        hs = _rmsnorm(xs, norm_ffn[i]) * (1.0 + ms[:, 4]) + ms[:, 3]
        h_all = jnp.concatenate([hp.reshape(n_p, D_MODEL), hs.reshape(Bd * S, D_MODEL)], axis=0)
        y_all = _moe(h_all, moe_w_router[i], moe_b_router[i], i, moe_w1, moe_b1, moe_w2, moe_b2)
        xp = xp + mp[:, 5] * y_all[:n_p].reshape(B, T, D_MODEL)
        xs = xs + ms[:, 5] * y_all[n_p:].reshape(Bd, S, D_MODEL)
    y_prompt = _rmsnorm(xp, norm_final)
    y_sample = _rmsnorm(xs, norm_final)
    stack = lambda sts, k: jnp.stack([st[k] for st in sts])
    return (y_prompt, y_sample, stack(nsa_new, 0), stack(nsa_new, 1), stack(nsa_new, 2), stack(nsa_new, 3),
            stack(nsa_new, 4), stack(nsa_new, 5), stack(swa_new, 0), stack(swa_new, 1),
            stack(gla_new, 0), stack(gla_new, 1))
```

```python
import functools

import jax
import jax.numpy as jnp
from jax import lax
from jax.experimental import pallas as pl
from jax.experimental.pallas import tpu as pltpu

D_MODEL = 2048
DEPTH = 4
PAST_LEN = 16384
PAGE_SIZE = 128
N_MIXERS = 3

NSA_HEADS = 16
NSA_KV = 2
NSA_HD = D_MODEL // NSA_HEADS
NSA_REP = NSA_HEADS // NSA_KV
NSA_BLOCK = 64
NSA_TOPN = 16
NSA_WINDOW = 512
NSA_SEL_QB = 64
SWA_HEADS = 32
SWA_KV = 4
SWA_HD = D_MODEL // SWA_HEADS
SWA_REP = SWA_HEADS // SWA_KV
SWA_WINDOW = 128
GLA_HEADS = 4
GLA_DK = D_MODEL // 2 // GLA_HEADS
GLA_DV = D_MODEL // GLA_HEADS
GLA_NORMALIZER = 16.0
GLA_CHUNK = 64
N_EXPERTS = 32
TOP_K = 4
D_FF = D_MODEL
SWIGLU_LIMIT = 7.0
SWIGLU_ALPHA = 1.702
QBLOCK = 128
NEG_INF = -1e30
EPS = 1e-6

VMEM_LIMIT_BYTES = 56 * 1024 * 1024
MOE_TM = 256
MOE_TF = 1024
MOE_TN = 1024
CAST_ROWS = 256


def _cast_rows(src_ref, dst_ref):
    n = dst_ref.shape[0] // CAST_ROWS

    def body(c, carry):
        r = pl.multiple_of(c * CAST_ROWS, CAST_ROWS)
        dst_ref[pl.ds(r, CAST_ROWS), :] = src_ref[pl.ds(r, CAST_ROWS), :].astype(jnp.bfloat16)
        return carry

    lax.fori_loop(0, n, body, 0)


def _moe_up_kernel(te_ref, tf_ref, tv_ref, x_ref, wg_ref, wu_ref, bg_ref, bu_ref, h_ref, wg_s, wu_s):
    i = pl.program_id(1)

    @pl.when(tf_ref[i] == 1)
    def _():
        _cast_rows(wg_ref, wg_s)
        _cast_rows(wu_ref, wu_s)

    @pl.when(tv_ref[i] == 1)
    def _():
        x = x_ref[...]
        g = jnp.dot(x, wg_s[...], preferred_element_type=jnp.float32) + bg_ref[...]
        u = jnp.dot(x, wu_s[...], preferred_element_type=jnp.float32) + bu_ref[...]
        g = jnp.minimum(g, SWIGLU_LIMIT)
        u = jnp.clip(u, -SWIGLU_LIMIT, SWIGLU_LIMIT)
        sig = 1.0 / (1.0 + jnp.exp(-SWIGLU_ALPHA * g))
        h_ref[...] = (g * sig * (u + 1.0)).astype(h_ref.dtype)

    @pl.when(tv_ref[i] == 0)
    def _():
        h_ref[...] = jnp.zeros_like(h_ref)


def _moe_down_kernel(te_ref, tf_ref, tv_ref, h_ref, w_ref, b_ref, gt_ref, y_ref, w_s):
    i = pl.program_id(1)

    @pl.when(tf_ref[i] == 1)
    def _():
        _cast_rows(w_ref, w_s)

    @pl.when(tv_ref[i] == 1)
    def _():
        y = jnp.dot(h_ref[...], w_s[...], preferred_element_type=jnp.float32) + b_ref[...]
        y_ref[...] = y * gt_ref[...]

    @pl.when(tv_ref[i] == 0)
    def _():
        y_ref[...] = jnp.zeros_like(y_ref)


def _moe_ffn(x_sorted, row_gate, tile_e, tile_first, tile_valid, layer, w1, b1, w2, b2):
    P, D = x_sorted.shape
    n_tiles = P // MOE_TM
    nj1 = D_FF // MOE_TF
    b1r = b1.reshape(DEPTH, N_EXPERTS, 1, 2 * D_FF)
    b2r = b2.reshape(DEPTH, N_EXPERTS, 1, D)
    params = pltpu.CompilerParams(dimension_semantics=("arbitrary", "arbitrary"),
                                  vmem_limit_bytes=VMEM_LIMIT_BYTES)
    h = pl.pallas_call(
        _moe_up_kernel,
        grid_spec=pltpu.PrefetchScalarGridSpec(
            num_scalar_prefetch=3,
            grid=(nj1, n_tiles),
            in_specs=[
                pl.BlockSpec((MOE_TM, D), lambda j, i, te, tf, tv: (i, 0)),
                pl.BlockSpec((None, None, D, MOE_TF), lambda j, i, te, tf, tv: (layer, te[i], 0, j)),
                pl.BlockSpec((None, None, D, MOE_TF), lambda j, i, te, tf, tv: (layer, te[i], 0, nj1 + j)),
                pl.BlockSpec((None, None, 1, MOE_TF), lambda j, i, te, tf, tv: (layer, te[i], 0, j)),
                pl.BlockSpec((None, None, 1, MOE_TF), lambda j, i, te, tf, tv: (layer, te[i], 0, nj1 + j)),
            ],
            out_specs=pl.BlockSpec((MOE_TM, MOE_TF), lambda j, i, te, tf, tv: (i, j)),
            scratch_shapes=[pltpu.VMEM((D, MOE_TF), jnp.bfloat16), pltpu.VMEM((D, MOE_TF), jnp.bfloat16)],
        ),
        out_shape=jax.ShapeDtypeStruct((P, D_FF), jnp.bfloat16),
        compiler_params=params,
        name="moe_up",
    )(tile_e, tile_first, tile_valid, x_sorted, w1, w1, b1r, b1r)
    y = pl.pallas_call(
        _moe_down_kernel,
        grid_spec=pltpu.PrefetchScalarGridSpec(
            num_scalar_prefetch=3,
            grid=(D // MOE_TN, n_tiles),
            in_specs=[
                pl.BlockSpec((MOE_TM, D_FF), lambda j, i, te, tf, tv: (i, 0)),
                pl.BlockSpec((None, None, D_FF, MOE_TN), lambda j, i, te, tf, tv: (layer, te[i], 0, j)),
                pl.BlockSpec((None, None, 1, MOE_TN), lambda j, i, te, tf, tv: (layer, te[i], 0, j)),
                pl.BlockSpec((MOE_TM, 1), lambda j, i, te, tf, tv: (i, 0)),
            ],
            out_specs=pl.BlockSpec((MOE_TM, MOE_TN), lambda j, i, te, tf, tv: (i, j)),
            scratch_shapes=[pltpu.VMEM((D_FF, MOE_TN), jnp.bfloat16)],
        ),
        out_shape=jax.ShapeDtypeStruct((P, D), jnp.float32),
        compiler_params=params,
        name="moe_down",
    )(tile_e, tile_first, tile_valid, h, w2, b2r, row_gate)
    return y


def _moe(x, w_r, b_r, layer, w1, b1, w2, b2):
    N, D = x.shape
    logits = (x @ w_r + b_r).astype(jnp.float32)
    top_v, top_i = lax.top_k(logits, TOP_K)
    gate = jax.nn.softmax(top_v, axis=-1)
    A = N * TOP_K
    e_flat = top_i.reshape(-1).astype(jnp.int32)
    onehot = (e_flat[:, None] == jnp.arange(N_EXPERTS, dtype=jnp.int32)[None, :]).astype(jnp.int32)
    csum = jnp.cumsum(onehot, axis=0)
    counts = csum[-1]
    rank = jnp.take_along_axis(csum, e_flat[:, None], axis=1)[:, 0] - 1
    padded = (counts + MOE_TM - 1) // MOE_TM * MOE_TM
    pend = jnp.cumsum(padded)
    pstart = pend - padded
    dest = pstart[e_flat] + rank
    n_tiles = -(-(A + N_EXPERTS * (MOE_TM - 1)) // MOE_TM)
    P = n_tiles * MOE_TM
    tile_start = jnp.arange(n_tiles, dtype=jnp.int32) * MOE_TM
    tile_valid = tile_start < pend[-1]
    tile_e = jnp.minimum(jnp.searchsorted(pend, tile_start, side='right'), N_EXPERTS - 1).astype(jnp.int32)
    prev_e = jnp.concatenate([jnp.full((1,), -1, jnp.int32), tile_e[:-1]])
    tile_first = ((tile_e != prev_e) & tile_valid).astype(jnp.int32)
    tok_flat = jnp.repeat(jnp.arange(N, dtype=jnp.int32), TOP_K)
    row_tok = jnp.full((P,), N, jnp.int32).at[dest].set(tok_flat)
    row_gate = jnp.zeros((P,), jnp.float32).at[dest].set(gate.reshape(-1))
    x_pad = jnp.concatenate([x.astype(jnp.bfloat16), jnp.zeros((1, D), jnp.bfloat16)], axis=0)
    x_sorted = x_pad[row_tok]
    y = _moe_ffn(x_sorted, row_gate[:, None], tile_e, tile_first, tile_valid.astype(jnp.int32), layer,
                 w1, b1, w2, b2)
    return y[dest.reshape(N, TOP_K)].sum(axis=1)


def _dot_nt(a, b):
    return lax.dot_general(a, b, (((1,), (1,)), ((), ())), preferred_element_type=jnp.float32)


def _compress_rows(load_slab, n_blocks, pe_ref, w1_ref, w2_ref, out_ref):
    W = 2 * NSA_HD
    for g in range(NSA_KV):
        def step(l, acc):
            x = jnp.concatenate([load_slab(2 * g, l), load_slab(2 * g + 1, l)], axis=1)
            x = (x + pe_ref[pl.ds(l, 1), :]).astype(jnp.bfloat16)
            return acc + jnp.dot(x, w1_ref[l], preferred_element_type=jnp.float32)

        hid = lax.fori_loop(0, NSA_BLOCK, step, jnp.zeros((n_blocks, W), jnp.float32))
        hid = hid * (1.0 / (1.0 + jnp.exp(-hid)))
        out_ref[:, g * W:(g + 1) * W] = jnp.dot(hid.astype(jnp.bfloat16), w2_ref[...],
                                                preferred_element_type=jnp.float32)


def _compress_weights(pe, w1, w2):
    z1 = jnp.zeros_like(w1[0])
    w1bd = jnp.concatenate([jnp.concatenate([w1[0], z1], axis=2), jnp.concatenate([z1, w1[1]], axis=2)], axis=1)
    z2 = jnp.zeros_like(w2[0])
    w2bd = jnp.concatenate([jnp.concatenate([w2[0], z2], axis=1), jnp.concatenate([z2, w2[1]], axis=1)], axis=0)
    return pe.reshape(NSA_BLOCK, 2 * NSA_HD), w1bd.astype(jnp.bfloat16), w2bd.astype(jnp.bfloat16)


CMP_DENSE_BLOCKS = 64
CMP_PAGES = 32


def _compress_dense_kernel(x0_ref, x1_ref, x2_ref, x3_ref, pe_ref, w1_ref, w2_ref, out_ref):
    xs = (x0_ref, x1_ref, x2_ref, x3_ref)
    n_blocks = out_ref.shape[0]
    load = lambda j, l: xs[j][pl.ds(l, n_blocks, stride=NSA_BLOCK), :]
    _compress_rows(load, n_blocks, pe_ref, w1_ref, w2_ref, out_ref)


def _nsa_compress_dense(kv_rows, pe2, w1bd, w2bd):
    R, W4 = kv_rows.shape
    nblk = R // NSA_BLOCK
    step = min(nblk, CMP_DENSE_BLOCKS)
    const = lambda shape: pl.BlockSpec(shape, lambda i: (0,) * len(shape))
    slab = lambda j: pl.BlockSpec((step * NSA_BLOCK, NSA_HD), lambda i: (i, j))
    return pl.pallas_call(
        _compress_dense_kernel,
        grid=(nblk // step,),
        in_specs=[slab(0), slab(1), slab(2), slab(3), const(pe2.shape), const(w1bd.shape), const(w2bd.shape)],
        out_specs=pl.BlockSpec((step, W4), lambda i: (i, 0)),
        out_shape=jax.ShapeDtypeStruct((nblk, W4), jnp.float32),
        compiler_params=pltpu.CompilerParams(dimension_semantics=("arbitrary",),
                                             vmem_limit_bytes=VMEM_LIMIT_BYTES),
        name="nsa_compress_dense",
    )(kv_rows, kv_rows, kv_rows, kv_rows, pe2, w1bd, w2bd)


def _page_slab_copy(pid_ref, cache_ref, buf, sem, layer, step, slot, p, j):
    pid = pid_ref[step * CMP_PAGES + p]
    return pltpu.make_async_copy(
        cache_ref.at[layer, pid, :, pl.ds(j * NSA_HD, NSA_HD)],
        buf.at[slot, j, pl.ds(p * PAGE_SIZE, PAGE_SIZE), :],
        sem.at[slot])


def _compress_paged_kernel(pid_ref, cache_ref, pe_ref, w1_ref, w2_ref, out_ref, buf, sem, *, layer):
    i = pl.program_id(0)
    n_steps = pl.num_programs(0)
    slot = lax.rem(i, 2)
    n_slabs = buf.shape[1]

    def for_pages(step, slot_, act):
        def body(p, carry):
            for j in range(n_slabs):
                act(_page_slab_copy(pid_ref, cache_ref, buf, sem, layer, step, slot_, p, j))
            return carry
        lax.fori_loop(0, CMP_PAGES, body, 0)

    @pl.when(i == 0)
    def _():
        for_pages(0, 0, lambda cp: cp.start())

    @pl.when(i + 1 < n_steps)
    def _():
        for_pages(i + 1, 1 - slot, lambda cp: cp.start())

    for_pages(i, slot, lambda cp: cp.wait())
    n_blocks = out_ref.shape[0]
    load = lambda j, l: buf[slot, j, pl.ds(l, n_blocks, stride=NSA_BLOCK), :]
    _compress_rows(load, n_blocks, pe_ref, w1_ref, w2_ref, out_ref)


def _nsa_compress_paged(cache, layer, page_ids, pe2, w1bd, w2bd):
    W4 = cache.shape[-1]
    n_pages = page_ids.shape[0]
    bpp = PAGE_SIZE // NSA_BLOCK
    const = lambda shape: pl.BlockSpec(shape, lambda i, pid: (0,) * len(shape))
    return pl.pallas_call(
        functools.partial(_compress_paged_kernel, layer=layer),
        grid_spec=pltpu.PrefetchScalarGridSpec(
            num_scalar_prefetch=1,
            grid=(n_pages // CMP_PAGES,),
            in_specs=[pl.BlockSpec(memory_space=pl.ANY), const(pe2.shape), const(w1bd.shape), const(w2bd.shape)],
            out_specs=pl.BlockSpec((CMP_PAGES * bpp, W4), lambda i, pid: (i, 0)),
            scratch_shapes=[pltpu.VMEM((2, W4 // NSA_HD, CMP_PAGES * PAGE_SIZE, NSA_HD), jnp.float32),
                            pltpu.SemaphoreType.DMA((2,))],
        ),
        out_shape=jax.ShapeDtypeStruct((n_pages * bpp, W4), jnp.float32),
        compiler_params=pltpu.CompilerParams(dimension_semantics=("arbitrary",),
                                             vmem_limit_bytes=VMEM_LIMIT_BYTES),
        name="nsa_compress_paged",
    )(page_ids, cache, pe2, w1bd, w2bd)


NSA_TQ = 128
NSA_TK = 512


def _alibi_slope(head_index, n_heads):
    return 2.0 ** (-8.0 * (head_index + 1) / n_heads)


def _nsa_prompt_kernel(q_ref, kc_ref, vc_ref, ks_ref, vs_ref, kw_ref, vw_ref, gl_ref, o_ref,
                       q_s, kc_s, vc_s, ks_s, vs_s, kw_s, vw_s, acc_s, m_s, l_s):
    g = pl.program_id(1)
    qi = pl.program_id(2)
    TQ, HD, R, L = NSA_TQ, NSA_HD, NSA_REP, NSA_BLOCK
    T = ks_ref.shape[1]
    nb = kc_ref.shape[1]
    scale = HD ** -0.5
    bf16 = jnp.bfloat16

    @pl.when(qi == 0)
    def _():
        ks_s[...] = ks_ref[0].astype(bf16)
        vs_s[...] = vs_ref[0].astype(bf16)
        kw_s[...] = kw_ref[0].astype(bf16)
        vw_s[...] = vw_ref[0].astype(bf16)
        kc_s[...] = jnp.zeros_like(kc_s)
        vc_s[...] = jnp.zeros_like(vc_s)
        kc_s[0:nb, :] = kc_ref[0].astype(bf16)
        vc_s[0:nb, :] = vc_ref[0].astype(bf16)

    for r in range(R):
        q_s[r] = q_ref[0, :, r * HD:(r + 1) * HD].astype(bf16)

    gate = 1.0 / (1.0 + jnp.exp(-gl_ref[0, 0]))
    slopes = [jnp.where(g == 0, _alibi_slope(r, NSA_HEADS), _alibi_slope(R + r, NSA_HEADS)) for r in range(R)]
    qpos0 = qi * TQ

    row_c = lax.broadcasted_iota(jnp.int32, (TQ, 128), 0) + qpos0
    col_c = lax.broadcasted_iota(jnp.int32, (TQ, 128), 1)
    cur_c = lax.shift_right_logical(row_c, 6)
    mask_c = col_c < cur_c
    dist_c = (row_c - (col_c * L + (L - 1))).astype(jnp.float32)
    imp = jnp.zeros((TQ, 128), jnp.float32)
    for r in range(R):
        s = _dot_nt(q_s[r], kc_s[...]) * scale - slopes[r] * dist_c
        s = jnp.where(mask_c, s, NEG_INF)
        p = jnp.exp(s - jnp.max(s, axis=-1, keepdims=True))
        p = p / jnp.sum(p, axis=-1, keepdims=True)
        p = jnp.where(mask_c, p, 0.0)
        imp = imp + p
        o = jnp.dot(p.astype(bf16), vc_s[...], preferred_element_type=jnp.float32)
        o_ref[0, :, r * HD:(r + 1) * HD] = gate[:, r:r + 1] * o

    valid = col_c <= cur_c
    forced = valid & ((col_c == 0) | (col_c >= cur_c - 1))
    score = jnp.where(forced, jnp.inf, jnp.where(valid, imp, -jnp.inf))
    sc_t = jnp.transpose(score)[0:32, :]
    jrow = lax.broadcasted_iota(jnp.int32, (32, TQ), 0)
    rank = jnp.zeros((32, TQ), jnp.int32)
    for i in range(32):
        row = sc_t[i:i + 1, :]
        ahead = (row > sc_t) | ((row == sc_t) & (jrow > i))
        rank = rank + ahead.astype(jnp.int32)
    sel_t = jnp.where(rank < NSA_TOPN, 1.0, 0.0)
    sel_t = jnp.concatenate([sel_t, jnp.zeros((128 - 32, TQ), jnp.float32)], axis=0)
    sel = jnp.transpose(sel_t).astype(bf16)

    WK = NSA_WINDOW + TQ
    w0 = pl.multiple_of(jnp.maximum(qi - NSA_WINDOW // TQ, 0) * TQ, TQ)
    row_w = lax.broadcasted_iota(jnp.int32, (TQ, WK), 0) + qpos0
    rel_w = row_w - (lax.broadcasted_iota(jnp.int32, (TQ, WK), 1) + w0)
    ok_w = (rel_w >= 0) & (rel_w < NSA_WINDOW)
    relf_w = rel_w.astype(jnp.float32)
    kw = kw_s[pl.ds(w0, WK), :]
    vw = vw_s[pl.ds(w0, WK), :]
    for r in range(R):
        s = _dot_nt(q_s[r], kw) * scale - slopes[r] * relf_w
        s = jnp.where(ok_w, s, NEG_INF)
        p = jnp.exp(s - jnp.max(s, axis=-1, keepdims=True))
        l = jnp.sum(p, axis=-1, keepdims=True)
        o = jnp.dot(p.astype(bf16), vw, preferred_element_type=jnp.float32) / l
        o_ref[0, :, r * HD:(r + 1) * HD] += gate[:, 2 * R + r:2 * R + r + 1] * o

    m_s[...] = jnp.full_like(m_s, -jnp.inf)
    l_s[...] = jnp.zeros_like(l_s)
    acc_s[...] = jnp.zeros_like(acc_s)
    TK = NSA_TK
    n_chunks = (qpos0 + TQ + TK - 1) // TK

    def chunk(c, carry):
        k0 = pl.multiple_of(c * TK, TK)
        row_t = lax.broadcasted_iota(jnp.int32, (TQ, TK), 0) + qpos0
        rel = row_t - (lax.broadcasted_iota(jnp.int32, (TQ, TK), 1) + k0)
        blk = lax.shift_right_logical(lax.broadcasted_iota(jnp.int32, (128, TK), 1) + k0, 6)
        expand = jnp.where(lax.broadcasted_iota(jnp.int32, (128, TK), 0) == blk, 1.0, 0.0).astype(bf16)
        picked = jnp.dot(sel, expand, preferred_element_type=jnp.float32)
        ok = (picked > 0.5) & (rel >= 0)
        relf = rel.astype(jnp.float32)
        kc = ks_s[pl.ds(k0, TK), :]
        vc = vs_s[pl.ds(k0, TK), :]
        for r in range(R):
            s = _dot_nt(q_s[r], kc) * scale - slopes[r] * relf
            s = jnp.where(ok, s, NEG_INF)
            m_old = m_s[r]
            m_new = jnp.maximum(m_old, jnp.max(s, axis=-1, keepdims=True))
            alpha = jnp.exp(m_old - m_new)
            p = jnp.exp(s - m_new)
            l_s[r] = alpha * l_s[r] + jnp.sum(p, axis=-1, keepdims=True)
            acc_s[r] = alpha * acc_s[r] + jnp.dot(p.astype(bf16), vc, preferred_element_type=jnp.float32)
            m_s[r] = m_new
        return carry

    lax.fori_loop(0, n_chunks, chunk, 0)
    for r in range(R):
        o_ref[0, :, r * HD:(r + 1) * HD] += gate[:, R + r:R + r + 1] * (acc_s[r] / l_s[r])


def _nsa_prompt_attention(p, cmp, gate_logits):
    B, T, _ = p.shape
    nb = cmp.shape[1]
    HD, R, G, TQ = NSA_HD, NSA_REP, NSA_KV, NSA_TQ
    kv0 = NSA_HEADS
    lane_kv = lambda branch, c: (lambda b, g, i: (b, 0, kv0 + branch * 2 * G + g * 2 + c))
    bf16 = jnp.bfloat16
    return pl.pallas_call(
        _nsa_prompt_kernel,
        grid=(B, G, T // TQ),
        in_specs=[
            pl.BlockSpec((1, TQ, R * HD), lambda b, g, i: (b, i, g)),
            pl.BlockSpec((1, nb, HD), lambda b, g, i: (b, 0, 2 * g)),
            pl.BlockSpec((1, nb, HD), lambda b, g, i: (b, 0, 2 * g + 1)),
            pl.BlockSpec((1, T, HD), lane_kv(1, 0)),
            pl.BlockSpec((1, T, HD), lane_kv(1, 1)),
            pl.BlockSpec((1, T, HD), lane_kv(2, 0)),
            pl.BlockSpec((1, T, HD), lane_kv(2, 1)),
            pl.BlockSpec((1, 1, TQ, 3 * R), lambda b, g, i: (b, g, i, 0)),
        ],
        out_specs=pl.BlockSpec((1, TQ, R * HD), lambda b, g, i: (b, i, g)),
        out_shape=jax.ShapeDtypeStruct((B, T, NSA_HEADS * HD), jnp.float32),
        scratch_shapes=[
            pltpu.VMEM((R, TQ, HD), bf16),
            pltpu.VMEM((128, HD), bf16), pltpu.VMEM((128, HD), bf16),
            pltpu.VMEM((T, HD), bf16), pltpu.VMEM((T, HD), bf16),
            pltpu.VMEM((T, HD), bf16), pltpu.VMEM((T, HD), bf16),
            pltpu.VMEM((R, TQ, HD), jnp.float32),
            pltpu.VMEM((R, TQ, 1), jnp.float32), pltpu.VMEM((R, TQ, 1), jnp.float32),
        ],
        compiler_params=pltpu.CompilerParams(dimension_semantics=("arbitrary", "arbitrary", "arbitrary"),
                                             vmem_limit_bytes=VMEM_LIMIT_BYTES),
        name="nsa_prompt_attention",
    )(p, cmp, cmp, p, p, p, p, gate_logits)


def _rmsnorm(x, g):
    xf = x.astype(jnp.float32)
    y = xf * lax.rsqrt(jnp.mean(xf * xf, axis=-1, keepdims=True) + EPS)
    return y.astype(x.dtype) * g


def _alibi_slopes(n_heads, n_groups):
    h = jnp.arange(1, n_heads + 1, dtype=jnp.float32)
    return jnp.exp2(-8.0 * h / n_heads).reshape(n_groups, n_heads // n_groups)


def _adaln(c, w, b):
    m = jax.nn.silu(c) @ w + b
    return m.reshape(c.shape[0], 6, 1, D_MODEL)


def _attend(q, k, v, mask, dist, slopes, sink=None):
    s = jnp.einsum('...qgrd,...kgd->...qgrk', q, k).astype(jnp.float32) * (q.shape[-1] ** -0.5)
    s = s - slopes[:, :, None] * dist[..., :, None, None, :]
    m = mask[..., :, None, None, :]
    s = jnp.where(m, s, NEG_INF)
    if sink is not None:
        sk = jnp.broadcast_to(sink.astype(jnp.float32)[:, :, None], s.shape[:-1] + (1,))
        p = jax.nn.softmax(jnp.concatenate([s, sk], axis=-1), axis=-1)[..., :-1]
    else:
        p = jax.nn.softmax(s, axis=-1)
    p = jnp.where(m, p, 0.0)
    o = jnp.einsum('...qgrk,...kgd->...qgrd', p.astype(v.dtype), v)
    return o, p


def _window_prompt(q, kv, window, slopes, sink=None):
    B, T = q.shape[:2]
    nqb = T // QBLOCK
    n_prev = -(-(window - 1) // QBLOCK)
    kb = (n_prev + 1) * QBLOCK
    pad = jnp.pad(kv, [(0, 0), (n_prev * QBLOCK, 0)] + [(0, 0)] * (kv.ndim - 2))
    blocks = pad.reshape((B, nqb + n_prev, QBLOCK) + kv.shape[2:])
    band = jnp.concatenate([blocks[:, j:j + nqb] for j in range(n_prev + 1)], axis=2)
    qpos = jnp.arange(T).reshape(nqb, QBLOCK)
    kpos = (jnp.arange(nqb)[:, None] - n_prev) * QBLOCK + jnp.arange(kb)[None, :]
    rel = qpos[:, :, None] - kpos[:, None, :]
    mask = (kpos[:, None, :] >= 0) & (rel >= 0) & (rel < window)
    qb = q.reshape((B, nqb, QBLOCK) + q.shape[2:])
    o, _ = _attend(qb, band[..., 0, :], band[..., 1, :], mask, rel.astype(jnp.float32), slopes, sink)
    return o.reshape(q.shape)


def _window_sample(q, kv_new, buf, window, slopes, sink=None):
    S = q.shape[1]
    wb = buf.shape[1]
    kv = jnp.concatenate([buf.astype(kv_new.dtype), kv_new], axis=1)
    qpos = PAST_LEN + jnp.arange(S)
    kpos = PAST_LEN - wb + jnp.arange(wb + S)
    rel = qpos[:, None] - kpos[None, :]
    mask = (rel >= 0) & (rel < window)
    o, _ = _attend(q, kv[..., 0, :], kv[..., 1, :], mask, rel.astype(jnp.float32), slopes, sink)
    return o, kv[:, S:]


def _sel_attend(q, k, v, kpos, qpos, slopes):
    s = jnp.einsum('bqgrd,bqgkd->bqgrk', q, k).astype(jnp.float32) * (q.shape[-1] ** -0.5)
    rel = qpos[None, :, None, None] - kpos
    s = s - slopes[None, None, :, :, None] * rel[:, :, :, None, :].astype(jnp.float32)
    m = (rel >= 0)[:, :, :, None, :]
    p = jnp.where(m, jax.nn.softmax(jnp.where(m, s, NEG_INF), axis=-1), 0.0)
    return jnp.einsum('bqgrk,bqgkd->bqgrd', p.astype(v.dtype), v)


def _nsa_project(h, w_in):
    B, T = h.shape[:2]
    p = h @ w_in
    nq = NSA_HEADS * NSA_HD
    nkv = 2 * NSA_KV * NSA_HD
    q = p[..., :nq].reshape(B, T, NSA_KV, NSA_REP, NSA_HD)
    kv = p[..., nq:nq + 3 * nkv].reshape(B, T, 3, NSA_KV, 2, NSA_HD)
    gates = jax.nn.sigmoid(p[..., nq + 3 * nkv:].reshape(B, T, 3, NSA_KV, NSA_REP))
    return q, kv[:, :, 0], kv[:, :, 1], kv[:, :, 2], gates


def _nsa_compress(blocks, pe, w1, w2):
    x = blocks + pe[:, None]
    hid = jax.nn.silu(jnp.einsum('...nlgcd,cldh->...ngch', x, w1))
    return jnp.einsum('...ngch,chd->...ngcd', hid, w2)


def _nsa_cmp_attend(q, cmp, qpos, slopes):
    nb = cmp.shape[-4]
    j = jnp.arange(nb)
    mask = j[None, :] < (qpos // NSA_BLOCK)[:, None]
    dist = (qpos[:, None] - (j[None, :] * NSA_BLOCK + NSA_BLOCK - 1)).astype(jnp.float32)
    o, p = _attend(q, cmp[..., 0, :], cmp[..., 1, :], mask, dist, slopes)
    return o, p.sum(axis=-2)


def _nsa_merge(gates, o_c, o_s, o_w, w_out):
    o = gates[:, :, 0, :, :, None] * o_c + gates[:, :, 1, :, :, None] * o_s + gates[:, :, 2, :, :, None] * o_w
    return o.reshape(o.shape[0], o.shape[1], -1) @ w_out


def _nsa_layer(hp, hs, cache_cmp, cache_slc, win_buf, page_table, li, w_in, pe, w1, w2, w_out, slopes):
    L = NSA_BLOCK
    B, T = hp.shape[:2]
    nq = NSA_HEADS * NSA_HD
    nkv = 2 * NSA_KV * NSA_HD
    kv_shape = (NSA_KV, 2, NSA_HD)
    pe2, w1bd, w2bd = _compress_weights(pe, w1, w2)
    pp = hp @ w_in
    kvc_rows = pp[..., nq:nq + nkv]
    kvc = kvc_rows.reshape((B, T) + kv_shape)
    kvs = pp[..., nq + nkv:nq + 2 * nkv].reshape((B, T) + kv_shape)
    win_p = pp[:, T - min(NSA_WINDOW, T):, nq + 2 * nkv:nq + 3 * nkv].reshape((B, min(NSA_WINDOW, T)) + kv_shape)
    cmp_p = _nsa_compress_dense(kvc_rows.reshape(B * T, nkv), pe2, w1bd, w2bd).reshape(B, T // L, nkv)
    gl = pp[..., nq + 3 * nkv:].reshape(B, T, 3, NSA_KV, NSA_REP).transpose(0, 3, 1, 2, 4)
    o_p = _nsa_prompt_attention(pp, cmp_p, gl.reshape(B, NSA_KV, T, 3 * NSA_REP))
    yp = o_p @ w_out

    Bd, S = hs.shape[:2]
    q, kvc_s, kvs_s, kvw_s, gates = _nsa_project(hs, w_in)
    nbp = PAST_LEN // L
    qpos = PAST_LEN + jnp.arange(S)
    cache_rows = cache_cmp.reshape(cache_cmp.shape[:3] + (nkv,))
    cmp = _nsa_compress_paged(cache_rows, li, page_table.reshape(-1), pe2, w1bd, w2bd)
    cmp = cmp.reshape((Bd, nbp) + kv_shape)
    o_c, imp = _nsa_cmp_attend(q, cmp, qpos, slopes)
    cur = (qpos // L)[:, None]
    j = jnp.arange(nbp)[None, :]
    valid = j < cur
    forced = valid & ((j == 0) | (j == cur - 1))
    score = jnp.where(forced[:, None, :], jnp.inf, jnp.where(valid[:, None, :], imp, -jnp.inf))
    _, sel = lax.top_k(score, min(NSA_TOPN - 1, nbp))
    bpp = PAGE_SIZE // L
    b_idx = jnp.arange(Bd)[:, None, None, None]
    phys = page_table[b_idx, sel // bpp]
    rows = (sel % bpp)[..., None] * L + jnp.arange(L)
    kv_past = cache_slc[li, phys[..., None], rows, jnp.arange(NSA_KV)[None, None, :, None, None]]
    kv_past = kv_past.reshape(sel.shape[:3] + (-1, 2, NSA_HD)).astype(kvs_s.dtype)
    kpos_past = (sel[..., None] * L + jnp.arange(L)).reshape(sel.shape[:3] + (-1,))
    kv_cur = jnp.broadcast_to(kvs_s.transpose(0, 2, 1, 3, 4)[:, None], (Bd, S, NSA_KV, S, 2, NSA_HD))
    kv_sel = jnp.concatenate([kv_past, kv_cur], axis=3)
    kpos = jnp.concatenate([kpos_past, jnp.broadcast_to(qpos, (Bd, S, NSA_KV, S))], axis=3)
    o_s = _sel_attend(q, kv_sel[..., 0, :], kv_sel[..., 1, :], kpos, qpos, slopes)
    o_w, win_s = _window_sample(q, kvw_s, win_buf, NSA_WINDOW, slopes)
    ys = _nsa_merge(gates, o_c, o_s, o_w, w_out)
    return yp, ys, (kvc, kvc_s, kvs, kvs_s, win_p, win_s)


def _swa_project(h, w_in, b_in):
    B, T = h.shape[:2]
    p = h @ w_in + b_in
    nq = SWA_HEADS * SWA_HD
    q = p[..., :nq].reshape(B, T, SWA_KV, SWA_REP, SWA_HD)
    kv = p[..., nq:].reshape(B, T, SWA_KV, 2, SWA_HD)
    return q, kv


def _swa_layer(hp, hs, buf, w_in, b_in, sinks, w_out, b_out, slopes):
    sink = sinks.reshape(SWA_KV, SWA_REP)
    B, T = hp.shape[:2]
    q, kv = _swa_project(hp, w_in, b_in)
    o = _window_prompt(q, kv, SWA_WINDOW, slopes, sink)
    yp = o.reshape(B, T, -1) @ w_out + b_out
    buf_p = kv[:, T - min(SWA_WINDOW, T):]
    Bd, S = hs.shape[:2]
    q, kv_s = _swa_project(hs, w_in, b_in)
    o, buf_s = _window_sample(q, kv_s, buf, SWA_WINDOW, slopes, sink)
    ys = o.reshape(Bd, S, -1) @ w_out + b_out
    return yp, ys, (buf_p, buf_s)


def _gla_project(h, w_in, w_a2, b_a):
    B, T = h.shape[:2]
    p = h @ w_in
    nk = GLA_HEADS * GLA_DK
    nv = GLA_HEADS * GLA_DV
    q = p[..., :nk].reshape(B, T, GLA_HEADS, GLA_DK).astype(jnp.float32) * (GLA_DK ** -0.5)
    k = p[..., nk:2 * nk].reshape(B, T, GLA_HEADS, GLA_DK).astype(jnp.float32)
    v = p[..., 2 * nk:2 * nk + nv].reshape(B, T, GLA_HEADS, GLA_DV).astype(jnp.float32)
    r = p[..., 2 * nk + nv:2 * nk + 2 * nv].reshape(B, T, GLA_HEADS, GLA_DV)
    a = (p[..., 2 * nk + 2 * nv:] @ w_a2 + b_a).astype(jnp.float32)
    g = (jax.nn.log_sigmoid(a) / GLA_NORMALIZER).reshape(B, T, GLA_HEADS, GLA_DK)
    return q, k, v, g, r


def _gla_chunk(state, q, k, v, g):
    C = q.shape[1]
    b = jnp.cumsum(g, axis=1)
    causal = jnp.tril(jnp.ones((C, C), dtype=bool))
    o_inter = jnp.einsum('bthk,bhkv->bthv', q * jnp.exp(b), state)
    diff = b[:, :, None] - b[:, None, :]
    decay = jnp.exp(jnp.where(causal[None, :, :, None, None], diff, -jnp.inf))
    a = jnp.einsum('bthk,btshk,bshk->btsh', q, decay, k)
    o_intra = jnp.einsum('btsh,bshv->bthv', a, v)
    b_last = b[:, -1]
    new_state = jnp.exp(b_last)[..., None] * state + jnp.einsum('bshk,bshv->bhkv', k * jnp.exp(b_last[:, None] - b), v)
    return new_state, o_inter + o_intra


def _gla_layer(hp, hs, state, w_in, w_a2, b_a, norm, w_out):
    def readout(o, r, h):
        y = _rmsnorm(o, norm).astype(h.dtype) * jax.nn.silu(r)
        return y.reshape(h.shape[0], h.shape[1], -1) @ w_out

    B, T = hp.shape[:2]
    q, k, v, g, r = _gla_project(hp, w_in, w_a2, b_a)
    nc = T // GLA_CHUNK
    chunks = lambda a: a.reshape((B, nc, GLA_CHUNK) + a.shape[2:]).swapaxes(0, 1)
    s0 = jnp.zeros((B, GLA_HEADS, GLA_DK, GLA_DV), jnp.float32)
    s_p, o = lax.scan(lambda s, xs: _gla_chunk(s, *xs), s0, (chunks(q), chunks(k), chunks(v), chunks(g)))
    o = o.swapaxes(0, 1).reshape(B, T, GLA_HEADS, GLA_DV)
    yp = readout(o, r, hp)
    q, k, v, g, r = _gla_project(hs, w_in, w_a2, b_a)
    s_s, o = _gla_chunk(state.astype(jnp.float32), q, k, v, g)
    ys = readout(o, r, hs)
    return yp, ys, (s_p.astype(hp.dtype), s_s.astype(state.dtype))


def kernel(x_prompt, x_sample, cache_nsa_cmp, cache_nsa_slc, state_nsa_win, state_swa_kv, state_gla,
           page_table, c_prompt, c_sample, ada_w, ada_b, norm_mix, norm_ffn, norm_final,
           nsa_w_in, nsa_cmp_pe, nsa_cmp_w1, nsa_cmp_w2, nsa_w_out,
           swa_w_in, swa_b_in, swa_sinks, swa_w_out, swa_b_out,
           gla_w_in, gla_w_a2, gla_b_a, gla_norm, gla_w_out,
           moe_w_router, moe_b_router, moe_w1, moe_b1, moe_w2, moe_b2):
    B, T = x_prompt.shape[:2]
    Bd, S = x_sample.shape[:2]
    n_p = B * T
    nsa_slopes = _alibi_slopes(NSA_HEADS, NSA_KV)
    swa_slopes = _alibi_slopes(SWA_HEADS, SWA_KV)
    xp, xs = x_prompt, x_sample
    nsa_new, swa_new, gla_new = [], [], []
    for i in range(DEPTH):
        mp = _adaln(c_prompt, ada_w[i], ada_b[i])
        ms = _adaln(c_sample, ada_w[i], ada_b[i])
        hp = _rmsnorm(xp, norm_mix[i]) * (1.0 + mp[:, 1]) + mp[:, 0]
        hs = _rmsnorm(xs, norm_mix[i]) * (1.0 + ms[:, 1]) + ms[:, 0]
        kind, li = i % N_MIXERS, i // N_MIXERS
        if kind == 0:
            yp, ys, st = _nsa_layer(hp, hs, cache_nsa_cmp, cache_nsa_slc, state_nsa_win[li], page_table, li,
                                    nsa_w_in[li], nsa_cmp_pe[li], nsa_cmp_w1[li], nsa_cmp_w2[li], nsa_w_out[li],
                                    nsa_slopes)
            nsa_new.append(st)
        elif kind == 1:
            yp, ys, st = _swa_layer(hp, hs, state_swa_kv[li], swa_w_in[li], swa_b_in[li], swa_sinks[li],
                                    swa_w_out[li], swa_b_out[li], swa_slopes)
            swa_new.append(st)
        else:
            yp, ys, st = _gla_layer(hp, hs, state_gla[li], gla_w_in[li], gla_w_a2[li], gla_b_a[li],
                                    gla_norm[li], gla_w_out[li])
            gla_new.append(st)
        xp = xp + mp[:, 2] * yp
        xs = xs + ms[:, 2] * ys
        hp = _rmsnorm(xp, norm_ffn[i]) * (1.0 + mp[:, 4]) + mp[:, 3]
        hs = _rmsnorm(xs, norm_ffn[i]) * (1.0 + ms[:, 4]) + ms[:, 3]
        h_all = jnp.concatenate([hp.reshape(n_p, D_MODEL), hs.reshape(Bd * S, D_MODEL)], axis=0)
        y_all = _moe(h_all, moe_w_router[i], moe_b_router[i], i, moe_w1, moe_b1, moe_w2, moe_b2)
        xp = xp + mp[:, 5] * y_all[:n_p].reshape(B, T, D_MODEL)
        xs = xs + ms[:, 5] * y_all[n_p:].reshape(Bd, S, D_MODEL)
    y_prompt = _rmsnorm(xp, norm_final)
    y_sample = _rmsnorm(xs, norm_final)
    stack = lambda sts, k: jnp.stack([st[k] for st in sts])
    return (y_prompt, y_sample, stack(nsa_new, 0), stack(nsa_new, 1), stack(nsa_new, 2), stack(nsa_new, 3),
            stack(nsa_new, 4), stack(nsa_new, 5), stack(swa_new, 0), stack(swa_new, 1),
            stack(gla_new, 0), stack(gla_new, 1))
```

```python
import functools

import jax
import jax.numpy as jnp
from jax import lax
from jax.experimental import pallas as pl
from jax.experimental.pallas import tpu as pltpu

D_MODEL = 2048
DEPTH = 4
PAST_LEN = 16384
PAGE_SIZE = 128
N_MIXERS = 3

NSA_HEADS = 16
NSA_KV = 2
NSA_HD = D_MODEL // NSA_HEADS
NSA_REP = NSA_HEADS // NSA_KV
NSA_BLOCK = 64
NSA_TOPN = 16
NSA_WINDOW = 512
NSA_SEL_QB = 64
SWA_HEADS = 32
SWA_KV = 4
SWA_HD = D_MODEL // SWA_HEADS
SWA_REP = SWA_HEADS // SWA_KV
SWA_WINDOW = 128
GLA_HEADS = 4
GLA_DK = D_MODEL // 2 // GLA_HEADS
GLA_DV = D_MODEL // GLA_HEADS
GLA_NORMALIZER = 16.0
GLA_CHUNK = 64
N_EXPERTS = 32
TOP_K = 4
D_FF = D_MODEL
SWIGLU_LIMIT = 7.0
SWIGLU_ALPHA = 1.702
QBLOCK = 128
NEG_INF = -1e30
EPS = 1e-6

VMEM_LIMIT_BYTES = 56 * 1024 * 1024
MOE_TM = 256
MOE_TF = 1024
MOE_TN = 1024
CAST_ROWS = 256


def _cast_rows(src_ref, dst_ref):
    n = dst_ref.shape[0] // CAST_ROWS

    def body(c, carry):
        r = pl.multiple_of(c * CAST_ROWS, CAST_ROWS)
        dst_ref[pl.ds(r, CAST_ROWS), :] = src_ref[pl.ds(r, CAST_ROWS), :].astype(jnp.bfloat16)
        return carry

    lax.fori_loop(0, n, body, 0)


def _moe_up_kernel(te_ref, tf_ref, tv_ref, x_ref, wg_ref, wu_ref, bg_ref, bu_ref, h_ref, wg_s, wu_s):
    i = pl.program_id(1)

    @pl.when(tf_ref[i] == 1)
    def _():
        _cast_rows(wg_ref, wg_s)
        _cast_rows(wu_ref, wu_s)

    @pl.when(tv_ref[i] == 1)
    def _():
        x = x_ref[...]
        g = jnp.dot(x, wg_s[...], preferred_element_type=jnp.float32) + bg_ref[...]
        u = jnp.dot(x, wu_s[...], preferred_element_type=jnp.float32) + bu_ref[...]
        g = jnp.minimum(g, SWIGLU_LIMIT)
        u = jnp.clip(u, -SWIGLU_LIMIT, SWIGLU_LIMIT)
        sig = 1.0 / (1.0 + jnp.exp(-SWIGLU_ALPHA * g))
        h_ref[...] = (g * sig * (u + 1.0)).astype(h_ref.dtype)

    @pl.when(tv_ref[i] == 0)
    def _():
        h_ref[...] = jnp.zeros_like(h_ref)


def _moe_down_kernel(te_ref, tf_ref, tv_ref, h_ref, w_ref, b_ref, gt_ref, y_ref, w_s):
    i = pl.program_id(1)

    @pl.when(tf_ref[i] == 1)
    def _():
        _cast_rows(w_ref, w_s)

    @pl.when(tv_ref[i] == 1)
    def _():
        y = jnp.dot(h_ref[...], w_s[...], preferred_element_type=jnp.float32) + b_ref[...]
        y_ref[...] = y * gt_ref[...]

    @pl.when(tv_ref[i] == 0)
    def _():
        y_ref[...] = jnp.zeros_like(y_ref)


def _moe_ffn(x_sorted, row_gate, tile_e, tile_first, tile_valid, layer, w1, b1, w2, b2):
    P, D = x_sorted.shape
    n_tiles = P // MOE_TM
    nj1 = D_FF // MOE_TF
    b1r = b1.reshape(DEPTH, N_EXPERTS, 1, 2 * D_FF)
    b2r = b2.reshape(DEPTH, N_EXPERTS, 1, D)
    params = pltpu.CompilerParams(dimension_semantics=("arbitrary", "arbitrary"),
                                  vmem_limit_bytes=VMEM_LIMIT_BYTES)
    h = pl.pallas_call(
        _moe_up_kernel,
        grid_spec=pltpu.PrefetchScalarGridSpec(
            num_scalar_prefetch=3,
            grid=(nj1, n_tiles),
            in_specs=[
                pl.BlockSpec((MOE_TM, D), lambda j, i, te, tf, tv: (i, 0)),
                pl.BlockSpec((None, None, D, MOE_TF), lambda j, i, te, tf, tv: (layer, te[i], 0, j)),
                pl.BlockSpec((None, None, D, MOE_TF), lambda j, i, te, tf, tv: (layer, te[i], 0, nj1 + j)),
                pl.BlockSpec((None, None, 1, MOE_TF), lambda j, i, te, tf, tv: (layer, te[i], 0, j)),
                pl.BlockSpec((None, None, 1, MOE_TF), lambda j, i, te, tf, tv: (layer, te[i], 0, nj1 + j)),
            ],
            out_specs=pl.BlockSpec((MOE_TM, MOE_TF), lambda j, i, te, tf, tv: (i, j)),
            scratch_shapes=[pltpu.VMEM((D, MOE_TF), jnp.bfloat16), pltpu.VMEM((D, MOE_TF), jnp.bfloat16)],
        ),
        out_shape=jax.ShapeDtypeStruct((P, D_FF), jnp.bfloat16),
        compiler_params=params,
        name="moe_up",
    )(tile_e, tile_first, tile_valid, x_sorted, w1, w1, b1r, b1r)
    y = pl.pallas_call(
        _moe_down_kernel,
        grid_spec=pltpu.PrefetchScalarGridSpec(
            num_scalar_prefetch=3,
            grid=(D // MOE_TN, n_tiles),
            in_specs=[
                pl.BlockSpec((MOE_TM, D_FF), lambda j, i, te, tf, tv: (i, 0)),
                pl.BlockSpec((None, None, D_FF, MOE_TN), lambda j, i, te, tf, tv: (layer, te[i], 0, j)),
                pl.BlockSpec((None, None, 1, MOE_TN), lambda j, i, te, tf, tv: (layer, te[i], 0, j)),
                pl.BlockSpec((MOE_TM, 1), lambda j, i, te, tf, tv: (i, 0)),
            ],
            out_specs=pl.BlockSpec((MOE_TM, MOE_TN), lambda j, i, te, tf, tv: (i, j)),
            scratch_shapes=[pltpu.VMEM((D_FF, MOE_TN), jnp.bfloat16)],
        ),
        out_shape=jax.ShapeDtypeStruct((P, D), jnp.float32),
        compiler_params=params,
        name="moe_down",
    )(tile_e, tile_first, tile_valid, h, w2, b2r, row_gate)
    return y


def _moe(x, w_r, b_r, layer, w1, b1, w2, b2):
    N, D = x.shape
    logits = (x @ w_r + b_r).astype(jnp.float32)
    top_v, top_i = lax.top_k(logits, TOP_K)
    gate = jax.nn.softmax(top_v, axis=-1)
    A = N * TOP_K
    e_flat = top_i.reshape(-1).astype(jnp.int32)
    onehot = (e_flat[:, None] == jnp.arange(N_EXPERTS, dtype=jnp.int32)[None, :]).astype(jnp.int32)
    csum = jnp.cumsum(onehot, axis=0)
    counts = csum[-1]
    rank = jnp.take_along_axis(csum, e_flat[:, None], axis=1)[:, 0] - 1
    padded = (counts + MOE_TM - 1) // MOE_TM * MOE_TM
    pend = jnp.cumsum(padded)
    pstart = pend - padded
    dest = pstart[e_flat] + rank
    n_tiles = -(-(A + N_EXPERTS * (MOE_TM - 1)) // MOE_TM)
    P = n_tiles * MOE_TM
    tile_start = jnp.arange(n_tiles, dtype=jnp.int32) * MOE_TM
    tile_valid = tile_start < pend[-1]
    tile_e = jnp.minimum(jnp.searchsorted(pend, tile_start, side='right'), N_EXPERTS - 1).astype(jnp.int32)
    prev_e = jnp.concatenate([jnp.full((1,), -1, jnp.int32), tile_e[:-1]])
    tile_first = ((tile_e != prev_e) & tile_valid).astype(jnp.int32)
    tok_flat = jnp.repeat(jnp.arange(N, dtype=jnp.int32), TOP_K)
    row_tok = jnp.full((P,), N, jnp.int32).at[dest].set(tok_flat)
    row_gate = jnp.zeros((P,), jnp.float32).at[dest].set(gate.reshape(-1))
    x_pad = jnp.concatenate([x.astype(jnp.bfloat16), jnp.zeros((1, D), jnp.bfloat16)], axis=0)
    x_sorted = x_pad[row_tok]
    y = _moe_ffn(x_sorted, row_gate[:, None], tile_e, tile_first, tile_valid.astype(jnp.int32), layer,
                 w1, b1, w2, b2)
    return y[dest.reshape(N, TOP_K)].sum(axis=1)


def _dot_nt(a, b):
    return lax.dot_general(a, b, (((1,), (1,)), ((), ())), preferred_element_type=jnp.float32)


N_SLABS = 2 * NSA_KV
SLAB_ROWS = NSA_BLOCK * N_SLABS
CMP_DENSE_BLOCKS = 64
CMP_PAGES = 32
CMP_UNROLL = 8


def _compress_rows(load_row, n_blocks, pe_ref, w1_ref, w2_ref, out_ref):
    W = 2 * NSA_HD
    for g in range(NSA_KV):
        def step(l, acc):
            x = jnp.concatenate([load_row(l, 2 * g), load_row(l, 2 * g + 1)], axis=1)
            x = (x + pe_ref[pl.ds(l, 1), :]).astype(jnp.bfloat16)
            return acc + jnp.dot(x, w1_ref[l], preferred_element_type=jnp.float32)

        hid = lax.fori_loop(0, NSA_BLOCK, step, jnp.zeros((n_blocks, W), jnp.float32), unroll=CMP_UNROLL)
        hid = hid * (1.0 / (1.0 + jnp.exp(-hid)))
        out_ref[:, g * W:(g + 1) * W] = jnp.dot(hid.astype(jnp.bfloat16), w2_ref[...],
                                                preferred_element_type=jnp.float32)


def _compress_weights(pe, w1, w2):
    z1 = jnp.zeros_like(w1[0])
    w1bd = jnp.concatenate([jnp.concatenate([w1[0], z1], axis=2), jnp.concatenate([z1, w1[1]], axis=2)], axis=1)
    z2 = jnp.zeros_like(w2[0])
    w2bd = jnp.concatenate([jnp.concatenate([w2[0], z2], axis=1), jnp.concatenate([z2, w2[1]], axis=1)], axis=0)
    return pe.reshape(NSA_BLOCK, 2 * NSA_HD), w1bd.astype(jnp.bfloat16), w2bd.astype(jnp.bfloat16)


def _compress_dense_kernel(x_ref, pe_ref, w1_ref, w2_ref, out_ref):
    n_blocks = out_ref.shape[0]
    load = lambda l, j: x_ref[pl.ds(l * N_SLABS + j, n_blocks, stride=SLAB_ROWS), :]
    _compress_rows(load, n_blocks, pe_ref, w1_ref, w2_ref, out_ref)


def _nsa_compress_dense(kv_slab_rows, pe2, w1bd, w2bd):
    nblk = kv_slab_rows.shape[0] // SLAB_ROWS
    step = min(nblk, CMP_DENSE_BLOCKS)
    W4 = N_SLABS * NSA_HD
    const = lambda shape: pl.BlockSpec(shape, lambda i: (0,) * len(shape))
    return pl.pallas_call(
        _compress_dense_kernel,
        grid=(nblk // step,),
        in_specs=[pl.BlockSpec((step * SLAB_ROWS, NSA_HD), lambda i: (i, 0)),
                  const(pe2.shape), const(w1bd.shape), const(w2bd.shape)],
        out_specs=pl.BlockSpec((step, W4), lambda i: (i, 0)),
        out_shape=jax.ShapeDtypeStruct((nblk, W4), jnp.float32),
        compiler_params=pltpu.CompilerParams(dimension_semantics=("arbitrary",),
                                             vmem_limit_bytes=VMEM_LIMIT_BYTES),
        name="nsa_compress_dense",
    )(kv_slab_rows, pe2, w1bd, w2bd)


def _page_copy(pid_ref, cache_ref, buf, sem, layer, step, slot, p):
    rows = cache_ref.shape[2]
    pid = pid_ref[step * CMP_PAGES + p]
    return pltpu.make_async_copy(cache_ref.at[layer, pid], buf.at[slot, pl.ds(p * rows, rows), :], sem.at[slot])


def _compress_paged_kernel(pid_ref, cache_ref, pe_ref, w1_ref, w2_ref, out_ref, buf, sem, *, layer):
    i = pl.program_id(0)
    n_steps = pl.num_programs(0)
    slot = lax.rem(i, 2)

    def for_pages(step, slot_, act):
        def body(p, carry):
            act(_page_copy(pid_ref, cache_ref, buf, sem, layer, step, slot_, p))
            return carry
        lax.fori_loop(0, CMP_PAGES, body, 0)

    @pl.when(i == 0)
    def _():
        for_pages(0, 0, lambda cp: cp.start())

    @pl.when(i + 1 < n_steps)
    def _():
        for_pages(i + 1, 1 - slot, lambda cp: cp.start())

    for_pages(i, slot, lambda cp: cp.wait())
    n_blocks = out_ref.shape[0]
    load = lambda l, j: buf[slot, pl.ds(l * N_SLABS + j, n_blocks, stride=SLAB_ROWS), :]
    _compress_rows(load, n_blocks, pe_ref, w1_ref, w2_ref, out_ref)


def _nsa_compress_paged(cache, layer, page_ids, pe2, w1bd, w2bd):
    page_rows = cache.shape[2]
    n_pages = page_ids.shape[0]
    bpp = page_rows // SLAB_ROWS
    W4 = N_SLABS * NSA_HD
    const = lambda shape: pl.BlockSpec(shape, lambda i, pid: (0,) * len(shape))
    return pl.pallas_call(
        functools.partial(_compress_paged_kernel, layer=layer),
        grid_spec=pltpu.PrefetchScalarGridSpec(
            num_scalar_prefetch=1,
            grid=(n_pages // CMP_PAGES,),
            in_specs=[pl.BlockSpec(memory_space=pl.ANY), const(pe2.shape), const(w1bd.shape), const(w2bd.shape)],
            out_specs=pl.BlockSpec((CMP_PAGES * bpp, W4), lambda i, pid: (i, 0)),
            scratch_shapes=[pltpu.VMEM((2, CMP_PAGES * page_rows, NSA_HD), jnp.float32),
                            pltpu.SemaphoreType.DMA((2,))],
        ),
        out_shape=jax.ShapeDtypeStruct((n_pages * bpp, W4), jnp.float32),
        compiler_params=pltpu.CompilerParams(dimension_semantics=("arbitrary",),
                                             vmem_limit_bytes=VMEM_LIMIT_BYTES),
        name="nsa_compress_paged",
    )(page_ids, cache, pe2, w1bd, w2bd)


NSA_TQ = 128
NSA_TK = 512
LOG2E = 1.4426950408889634
MASK_DIST = 1e30


def _alibi_slope(head_index, n_heads):
    return 2.0 ** (-8.0 * (head_index + 1) / n_heads)


def _nsa_prompt_kernel(q_ref, kc_ref, vc_ref, ks_ref, vs_ref, kw_ref, vw_ref, gl_ref, o_ref,
                       q_s, kc_s, vc_s, ks_s, vs_s, kw_s, vw_s, s_s, p_s, acc_s, m_s, l_s, a_s):
    g = pl.program_id(1)
    qi = pl.program_id(2)
    TQ, HD, R, L = NSA_TQ, NSA_HD, NSA_REP, NSA_BLOCK
    T = ks_ref.shape[1]
    nb = kc_ref.shape[1]
    scale = HD ** -0.5
    bf16 = jnp.bfloat16

    @pl.when(qi == 0)
    def _():
        ks_s[...] = ks_ref[0].astype(bf16)
        vs_s[...] = vs_ref[0].astype(bf16)
        kw_s[...] = kw_ref[0].astype(bf16)
        vw_s[...] = vw_ref[0].astype(bf16)
        kc_s[...] = jnp.zeros_like(kc_s)
        vc_s[...] = jnp.zeros_like(vc_s)
        kc_s[0:nb, :] = kc_ref[0].astype(bf16)
        vc_s[0:nb, :] = vc_ref[0].astype(bf16)

    for r in range(R):
        q_s[r * TQ:(r + 1) * TQ, :] = q_ref[0, :, r * HD:(r + 1) * HD].astype(bf16)

    gate = 1.0 / (1.0 + jnp.exp(-gl_ref[0, 0]))
    slopes = [jnp.where(g == 0, _alibi_slope(r, NSA_HEADS), _alibi_slope(R + r, NSA_HEADS)) for r in range(R)]
    qpos0 = qi * TQ

    row_c = lax.broadcasted_iota(jnp.int32, (TQ, 128), 0) + qpos0
    col_c = lax.broadcasted_iota(jnp.int32, (TQ, 128), 1)
    cur_c = lax.shift_right_logical(row_c, 6)
    mask_c = col_c < cur_c
    dist_c = (row_c - (col_c * L + (L - 1))).astype(jnp.float32)
    imp = jnp.zeros((TQ, 128), jnp.float32)
    s_s[:, 0:128] = _dot_nt(q_s[...], kc_s[...]) * scale
    for r in range(R):
        s = s_s[r * TQ:(r + 1) * TQ, 0:128] - slopes[r] * dist_c
        s = jnp.where(mask_c, s, NEG_INF)
        p = jnp.exp(s - jnp.max(s, axis=-1, keepdims=True))
        p = p / jnp.sum(p, axis=-1, keepdims=True)
        p = jnp.where(mask_c, p, 0.0)
        imp = imp + p
        o = jnp.dot(p.astype(bf16), vc_s[...], preferred_element_type=jnp.float32)
        o_ref[0, :, r * HD:(r + 1) * HD] = gate[:, r:r + 1] * o

    valid = col_c <= cur_c
    forced = valid & ((col_c == 0) | (col_c >= cur_c - 1))
    score = jnp.where(forced, jnp.inf, jnp.where(valid, imp, -jnp.inf))
    sc_t = jnp.transpose(score)[0:32, :]
    jrow = lax.broadcasted_iota(jnp.int32, (32, TQ), 0)
    rank = jnp.zeros((32, TQ), jnp.int32)
    for i in range(32):
        row = sc_t[i:i + 1, :]
        ahead = (row > sc_t) | ((row == sc_t) & (jrow > i))
        rank = rank + ahead.astype(jnp.int32)
    sel_t = jnp.where(rank < NSA_TOPN, 1.0, 0.0)
    sel_t = jnp.concatenate([sel_t, jnp.zeros((128 - 32, TQ), jnp.float32)], axis=0)
    sel = jnp.transpose(sel_t).astype(bf16)

    WK = NSA_WINDOW + TQ
    w0 = pl.multiple_of(jnp.maximum(qi - NSA_WINDOW // TQ, 0) * TQ, TQ)
    row_w = lax.broadcasted_iota(jnp.int32, (TQ, WK), 0) + qpos0
    rel_w = row_w - (lax.broadcasted_iota(jnp.int32, (TQ, WK), 1) + w0)
    ok_w = (rel_w >= 0) & (rel_w < NSA_WINDOW)
    relm_w = jnp.where(ok_w, rel_w.astype(jnp.float32), MASK_DIST)
    s_s[...] = _dot_nt(q_s[...], kw_s[pl.ds(w0, WK), :]) * (scale * LOG2E)
    for r in range(R):
        rows = slice(r * TQ, (r + 1) * TQ)
        s = s_s[rows, :] - (slopes[r] * LOG2E) * relm_w
        p = jnp.exp2(s - jnp.max(s, axis=-1, keepdims=True))
        l_s[rows, :] = jnp.sum(p, axis=-1, keepdims=True)
        p_s[rows, :] = p.astype(bf16)
    acc_s[...] = jnp.dot(p_s[...], vw_s[pl.ds(w0, WK), :], preferred_element_type=jnp.float32) / l_s[...]
    for r in range(R):
        rows = slice(r * TQ, (r + 1) * TQ)
        o_ref[0, :, r * HD:(r + 1) * HD] += gate[:, 2 * R + r:2 * R + r + 1] * acc_s[rows, :]

    m_s[...] = jnp.full_like(m_s, -jnp.inf)
    l_s[...] = jnp.zeros_like(l_s)
    acc_s[...] = jnp.zeros_like(acc_s)
    TK = NSA_TK
    n_chunks = (qpos0 + TQ + TK - 1) // TK

    def chunk(c, carry):
        k0 = pl.multiple_of(c * TK, TK)
        row_t = lax.broadcasted_iota(jnp.int32, (TQ, TK), 0) + qpos0
        rel = row_t - (lax.broadcasted_iota(jnp.int32, (TQ, TK), 1) + k0)
        blk = lax.shift_right_logical(lax.broadcasted_iota(jnp.int32, (128, TK), 1) + k0, 6)
        expand = jnp.where(lax.broadcasted_iota(jnp.int32, (128, TK), 0) == blk, 1.0, 0.0).astype(bf16)
        picked = jnp.dot(sel, expand, preferred_element_type=jnp.float32)
        ok = (picked > 0.5) & (rel >= 0)
        relm = jnp.where(ok, rel.astype(jnp.float32), MASK_DIST)
        s_s[:, 0:TK] = _dot_nt(q_s[...], ks_s[pl.ds(k0, TK), :]) * (scale * LOG2E)
        for r in range(R):
            rows = slice(r * TQ, (r + 1) * TQ)
            s = s_s[rows, 0:TK] - (slopes[r] * LOG2E) * relm
            m_old = m_s[rows, :]
            m_new = jnp.maximum(m_old, jnp.max(s, axis=-1, keepdims=True))
            alpha = jnp.exp2(m_old - m_new)
            p = jnp.exp2(s - m_new)
            l_s[rows, :] = alpha * l_s[rows, :] + jnp.sum(p, axis=-1, keepdims=True)
            p_s[rows, 0:TK] = p.astype(bf16)
            a_s[rows, :] = alpha
            m_s[rows, :] = m_new
        acc_s[...] = a_s[...] * acc_s[...] + jnp.dot(p_s[:, 0:TK], vs_s[pl.ds(k0, TK), :],
                                                     preferred_element_type=jnp.float32)
        return carry

    lax.fori_loop(0, n_chunks, chunk, 0)
    acc_s[...] = acc_s[...] / l_s[...]
    for r in range(R):
        rows = slice(r * TQ, (r + 1) * TQ)
        o_ref[0, :, r * HD:(r + 1) * HD] += gate[:, R + r:R + r + 1] * acc_s[rows, :]


def _nsa_prompt_attention(p, cmp, gate_logits):
    B, T, _ = p.shape
    nb = cmp.shape[1]
    HD, R, G, TQ = NSA_HD, NSA_REP, NSA_KV, NSA_TQ
    kv0 = NSA_HEADS
    lane_kv = lambda branch, c: (lambda b, g, i: (b, 0, kv0 + branch * 2 * G + g * 2 + c))
    bf16 = jnp.bfloat16
    return pl.pallas_call(
        _nsa_prompt_kernel,
        grid=(B, G, T // TQ),
        in_specs=[
            pl.BlockSpec((1, TQ, R * HD), lambda b, g, i: (b, i, g)),
            pl.BlockSpec((1, nb, HD), lambda b, g, i: (b, 0, 2 * g)),
            pl.BlockSpec((1, nb, HD), lambda b, g, i: (b, 0, 2 * g + 1)),
            pl.BlockSpec((1, T, HD), lane_kv(1, 0)),
            pl.BlockSpec((1, T, HD), lane_kv(1, 1)),
            pl.BlockSpec((1, T, HD), lane_kv(2, 0)),
            pl.BlockSpec((1, T, HD), lane_kv(2, 1)),
            pl.BlockSpec((1, 1, TQ, 3 * R), lambda b, g, i: (b, g, i, 0)),
        ],
        out_specs=pl.BlockSpec((1, TQ, R * HD), lambda b, g, i: (b, i, g)),
        out_shape=jax.ShapeDtypeStruct((B, T, NSA_HEADS * HD), jnp.float32),
        scratch_shapes=[
            pltpu.VMEM((R * TQ, HD), bf16),
            pltpu.VMEM((128, HD), bf16), pltpu.VMEM((128, HD), bf16),
            pltpu.VMEM((T, HD), bf16), pltpu.VMEM((T, HD), bf16),
            pltpu.VMEM((T, HD), bf16), pltpu.VMEM((T, HD), bf16),
            pltpu.VMEM((R * TQ, NSA_WINDOW + TQ), jnp.float32),
            pltpu.VMEM((R * TQ, NSA_WINDOW + TQ), bf16),
            pltpu.VMEM((R * TQ, HD), jnp.float32),
            pltpu.VMEM((R * TQ, 1), jnp.float32), pltpu.VMEM((R * TQ, 1), jnp.float32),
            pltpu.VMEM((R * TQ, 1), jnp.float32),
        ],
        compiler_params=pltpu.CompilerParams(dimension_semantics=("arbitrary", "arbitrary", "arbitrary"),
                                             vmem_limit_bytes=VMEM_LIMIT_BYTES),
        name="nsa_prompt_attention",
    )(p, cmp, cmp, p, p, p, p, gate_logits)


NSA_SEL_PAST = NSA_TOPN - 1
SEL_SLOTS = 1024
WIN_SLOTS = NSA_WINDOW + 128
NEW_ROWS = 16


def _row_slopes(g, n_rows):
    r = lax.broadcasted_iota(jnp.int32, (n_rows, 1), 0) % NSA_REP
    return jnp.exp2(-8.0 * (g * NSA_REP + r + 1).astype(jnp.float32) / NSA_HEADS)


def _nsa_sample_cmp_kernel(q_ref, kc_ref, vc_ref, oc_ref, sel_ref):
    g = pl.program_id(1)
    S = q_ref.shape[2] // NSA_REP
    nbp = kc_ref.shape[0]
    L = NSA_BLOCK
    n_rows = S * NSA_REP
    bf16 = jnp.bfloat16
    q = q_ref[0, 0].astype(bf16)
    row = lax.broadcasted_iota(jnp.int32, (n_rows, nbp), 0)
    col = lax.broadcasted_iota(jnp.int32, (n_rows, nbp), 1)
    qpos = PAST_LEN + row // NSA_REP
    cur = qpos // L
    mask = col < cur
    dist = (qpos - (col * L + (L - 1))).astype(jnp.float32)
    s = _dot_nt(q, kc_ref[...].astype(bf16)) * (NSA_HD ** -0.5) - _row_slopes(g, n_rows) * dist
    s = jnp.where(mask, s, NEG_INF)
    p = jnp.exp(s - jnp.max(s, axis=-1, keepdims=True))
    p = p / jnp.sum(p, axis=-1, keepdims=True)
    p = jnp.where(mask, p, 0.0)
    oc_ref[0, 0] = jnp.dot(p.astype(bf16), vc_ref[...].astype(bf16), preferred_element_type=jnp.float32)

    imp = jnp.sum(p.reshape(S, NSA_REP, nbp), axis=1)
    colS = lax.broadcasted_iota(jnp.int32, (S, nbp), 1)
    curS = (PAST_LEN + lax.broadcasted_iota(jnp.int32, (S, nbp), 0)) // L
    valid = colS < curS
    forced = valid & ((colS == 0) | (colS == curS - 1))
    score = jnp.where(forced, jnp.inf, jnp.where(valid, imp, -jnp.inf))
    score_t = jnp.transpose(jnp.concatenate([score, jnp.zeros((128 - S, nbp), jnp.float32)], axis=0))
    ii = lax.broadcasted_iota(jnp.int32, (nbp, nbp), 0)
    jj = lax.broadcasted_iota(jnp.int32, (nbp, nbp), 1)
    lane = lax.broadcasted_iota(jnp.int32, (1, 128), 1)
    jrow = lax.broadcasted_iota(jnp.int32, (1, nbp), 1).astype(jnp.float32)
    out_rows = []
    for si in range(S):
        c = score_t[:, si:si + 1]
        rw = score[si:si + 1, :]
        ahead = (c > rw) | ((c == rw) & (ii < jj))
        rank = jnp.sum(jnp.where(ahead, 1.0, 0.0), axis=0, keepdims=True)
        ids = jnp.zeros((1, 128), jnp.float32)
        for k in range(NSA_SEL_PAST):
            idx = jnp.sum(jnp.where(rank == k, jrow, 0.0), axis=1, keepdims=True)
            ids = jnp.where(lane == k, idx, ids)
        out_rows.append(ids.astype(jnp.int32))
    out_rows.append(jnp.zeros((8 - S, 128), jnp.int32))
    sel_ref[0, 0] = jnp.concatenate(out_rows, axis=0)


def _nsa_sample_cmp(q_rows, cmp_rows):
    Bd, G, n_rows, HD = q_rows.shape
    nbp = cmp_rows.shape[0] // Bd
    return pl.pallas_call(
        _nsa_sample_cmp_kernel,
        grid=(Bd, G),
        in_specs=[pl.BlockSpec((1, 1, n_rows, HD), lambda b, g: (b, g, 0, 0)),
                  pl.BlockSpec((nbp, HD), lambda b, g: (b, 2 * g)),
                  pl.BlockSpec((nbp, HD), lambda b, g: (b, 2 * g + 1))],
        out_specs=[pl.BlockSpec((1, 1, n_rows, HD), lambda b, g: (b, g, 0, 0)),
                   pl.BlockSpec((1, 1, 8, 128), lambda b, g: (b, g, 0, 0))],
        out_shape=[jax.ShapeDtypeStruct((Bd, G, n_rows, HD), jnp.float32),
                   jax.ShapeDtypeStruct((Bd, G, 8, 128), jnp.int32)],
        compiler_params=pltpu.CompilerParams(dimension_semantics=("arbitrary", "arbitrary"),
                                             vmem_limit_bytes=VMEM_LIMIT_BYTES),
        name="nsa_sample_cmp",
    )(q_rows, cmp_rows, cmp_rows)


def _nsa_sample_sel_kernel(phys_ref, sel_ref, q_ref, oc_ref, gate_ref, new_ref, win_ref, *rest):
    blk_refs = rest[:NSA_SEL_PAST]
    o_ref = rest[NSA_SEL_PAST]
    ks_s, vs_s, kw_s, vw_s = rest[NSA_SEL_PAST + 1:]
    b = pl.program_id(0)
    g = pl.program_id(1)
    s_idx = pl.program_id(2)
    S = pl.num_programs(2)
    R, L, HD = NSA_REP, NSA_BLOCK, NSA_HD
    bf16 = jnp.bfloat16
    n_past = NSA_SEL_PAST * L
    wb = win_ref.shape[0] // N_SLABS
    base = ((b * S + s_idx) * NSA_KV + g) * NSA_SEL_PAST
    scale = HD ** -0.5
    slopes = _row_slopes(g, R)
    qpos = PAST_LEN + s_idx

    @pl.when(s_idx == 0)
    def _():
        kw_s[...] = jnp.zeros_like(kw_s)
        vw_s[...] = jnp.zeros_like(vw_s)
        kw_s[0:wb, :] = win_ref[pl.ds(g * 2, wb, stride=N_SLABS), :].astype(bf16)
        vw_s[0:wb, :] = win_ref[pl.ds(g * 2 + 1, wb, stride=N_SLABS), :].astype(bf16)
        kw_s[wb:wb + NEW_ROWS, :] = new_ref[0, 0, 1, 0].astype(bf16)
        vw_s[wb:wb + NEW_ROWS, :] = new_ref[0, 0, 1, 1].astype(bf16)
        ks_s[n_past:, :] = jnp.zeros((SEL_SLOTS - n_past, HD), bf16)
        vs_s[n_past:, :] = jnp.zeros((SEL_SLOTS - n_past, HD), bf16)
        ks_s[n_past:n_past + NEW_ROWS, :] = new_ref[0, 0, 0, 0].astype(bf16)
        vs_s[n_past:n_past + NEW_ROWS, :] = new_ref[0, 0, 0, 1].astype(bf16)

    q = q_ref[0, 0].astype(bf16)
    gate = 1.0 / (1.0 + jnp.exp(-gate_ref[0, 0]))

    lane = lax.broadcasted_iota(jnp.int32, (1, SEL_SLOTS), 1)
    slot = lane // L
    kpos = jnp.where(lane >= n_past, PAST_LEN + (lane - n_past), lane % L)
    for k in range(NSA_SEL_PAST):
        ks_s[k * L:(k + 1) * L, :] = blk_refs[k][pl.ds(g * 2, L, stride=N_SLABS), :].astype(bf16)
        vs_s[k * L:(k + 1) * L, :] = blk_refs[k][pl.ds(g * 2 + 1, L, stride=N_SLABS), :].astype(bf16)
        kpos = kpos + jnp.where(slot == k, sel_ref[base + k] * L, 0)
    rel = qpos - kpos
    ok = (rel >= 0) & (lane < n_past + S)
    sc = _dot_nt(q, ks_s[...]) * scale - slopes * rel.astype(jnp.float32)
    sc = jnp.where(ok, sc, NEG_INF)
    p = jnp.exp(sc - jnp.max(sc, axis=-1, keepdims=True))
    p = p / jnp.sum(p, axis=-1, keepdims=True)
    p = jnp.where(ok, p, 0.0)
    o_s = jnp.dot(p.astype(bf16), vs_s[...], preferred_element_type=jnp.float32)

    lane_w = lax.broadcasted_iota(jnp.int32, (1, WIN_SLOTS), 1)
    kpos_w = jnp.where(lane_w >= wb, PAST_LEN + (lane_w - wb), PAST_LEN - wb + lane_w)
    rel_w = qpos - kpos_w
    ok_w = (rel_w >= 0) & (rel_w < NSA_WINDOW) & (lane_w < wb + S)
    sw = _dot_nt(q, kw_s[...]) * scale - slopes * rel_w.astype(jnp.float32)
    sw = jnp.where(ok_w, sw, NEG_INF)
    pw = jnp.exp(sw - jnp.max(sw, axis=-1, keepdims=True))
    pw = pw / jnp.sum(pw, axis=-1, keepdims=True)
    pw = jnp.where(ok_w, pw, 0.0)
    o_w = jnp.dot(pw.astype(bf16), vw_s[...], preferred_element_type=jnp.float32)

    o_ref[0, 0] = gate[:, 0:1] * oc_ref[0, 0] + gate[:, 1:2] * o_s + gate[:, 2:3] * o_w


def _nsa_sample_sel(phys_blk, sel_blk, q_rows, o_c, gate_rows, new_kv, win_rows, cache_blocks, layer):
    Bd, G, n_rows, HD = q_rows.shape
    S = n_rows // NSA_REP
    R = NSA_REP
    bf16 = jnp.bfloat16
    row_blk = lambda b, g, s, ph, se: (b, g, s, 0)

    def blk_spec(k):
        return pl.BlockSpec((None, None, SLAB_ROWS, HD),
                            lambda b, g, s, ph, se: (layer, ph[((b * S + s) * NSA_KV + g) * NSA_SEL_PAST + k], 0, 0))

    return pl.pallas_call(
        _nsa_sample_sel_kernel,
        grid_spec=pltpu.PrefetchScalarGridSpec(
            num_scalar_prefetch=2,
            grid=(Bd, G, S),
            in_specs=[pl.BlockSpec((1, 1, R, HD), row_blk),
                      pl.BlockSpec((1, 1, R, HD), row_blk),
                      pl.BlockSpec((1, 1, R, 3), row_blk),
                      pl.BlockSpec((1, 1, 2, 2, NEW_ROWS, HD), lambda b, g, s, ph, se: (b, g, 0, 0, 0, 0)),
                      pl.BlockSpec((None, None, win_rows.shape[2], HD), lambda b, g, s, ph, se: (layer, b, 0, 0))]
                     + [blk_spec(k) for k in range(NSA_SEL_PAST)],
            out_specs=pl.BlockSpec((1, 1, R, HD), row_blk),
            scratch_shapes=[pltpu.VMEM((SEL_SLOTS, HD), bf16), pltpu.VMEM((SEL_SLOTS, HD), bf16),
                            pltpu.VMEM((WIN_SLOTS, HD), bf16), pltpu.VMEM((WIN_SLOTS, HD), bf16)],
        ),
        out_shape=jax.ShapeDtypeStruct((Bd, G, n_rows, HD), jnp.float32),
        compiler_params=pltpu.CompilerParams(dimension_semantics=("arbitrary", "arbitrary", "arbitrary"),
                                             vmem_limit_bytes=VMEM_LIMIT_BYTES),
        name="nsa_sample_sel",
    )(phys_blk, sel_blk, q_rows, o_c, gate_rows, new_kv, win_rows, *([cache_blocks] * NSA_SEL_PAST))


def _rmsnorm(x, g):
    xf = x.astype(jnp.float32)
    y = xf * lax.rsqrt(jnp.mean(xf * xf, axis=-1, keepdims=True) + EPS)
    return y.astype(x.dtype) * g


def _alibi_slopes(n_heads, n_groups):
    h = jnp.arange(1, n_heads + 1, dtype=jnp.float32)
    return jnp.exp2(-8.0 * h / n_heads).reshape(n_groups, n_heads // n_groups)


def _adaln(c, w, b):
    m = jax.nn.silu(c) @ w + b
    return m.reshape(c.shape[0], 6, 1, D_MODEL)


def _attend(q, k, v, mask, dist, slopes, sink=None):
    s = jnp.einsum('...qgrd,...kgd->...qgrk', q, k).astype(jnp.float32) * (q.shape[-1] ** -0.5)
    s = s - slopes[:, :, None] * dist[..., :, None, None, :]
    m = mask[..., :, None, None, :]
    s = jnp.where(m, s, NEG_INF)
    if sink is not None:
        sk = jnp.broadcast_to(sink.astype(jnp.float32)[:, :, None], s.shape[:-1] + (1,))
        p = jax.nn.softmax(jnp.concatenate([s, sk], axis=-1), axis=-1)[..., :-1]
    else:
        p = jax.nn.softmax(s, axis=-1)
    p = jnp.where(m, p, 0.0)
    o = jnp.einsum('...qgrk,...kgd->...qgrd', p.astype(v.dtype), v)
    return o, p


def _window_prompt(q, kv, window, slopes, sink=None):
    B, T = q.shape[:2]
    nqb = T // QBLOCK
    n_prev = -(-(window - 1) // QBLOCK)
    kb = (n_prev + 1) * QBLOCK
    pad = jnp.pad(kv, [(0, 0), (n_prev * QBLOCK, 0)] + [(0, 0)] * (kv.ndim - 2))
    blocks = pad.reshape((B, nqb + n_prev, QBLOCK) + kv.shape[2:])
    band = jnp.concatenate([blocks[:, j:j + nqb] for j in range(n_prev + 1)], axis=2)
    qpos = jnp.arange(T).reshape(nqb, QBLOCK)
    kpos = (jnp.arange(nqb)[:, None] - n_prev) * QBLOCK + jnp.arange(kb)[None, :]
    rel = qpos[:, :, None] - kpos[:, None, :]
    mask = (kpos[:, None, :] >= 0) & (rel >= 0) & (rel < window)
    qb = q.reshape((B, nqb, QBLOCK) + q.shape[2:])
    o, _ = _attend(qb, band[..., 0, :], band[..., 1, :], mask, rel.astype(jnp.float32), slopes, sink)
    return o.reshape(q.shape)


def _window_sample(q, kv_new, buf, window, slopes, sink=None):
    S = q.shape[1]
    wb = buf.shape[1]
    kv = jnp.concatenate([buf.astype(kv_new.dtype), kv_new], axis=1)
    qpos = PAST_LEN + jnp.arange(S)
    kpos = PAST_LEN - wb + jnp.arange(wb + S)
    rel = qpos[:, None] - kpos[None, :]
    mask = (rel >= 0) & (rel < window)
    o, _ = _attend(q, kv[..., 0, :], kv[..., 1, :], mask, rel.astype(jnp.float32), slopes, sink)
    return o, kv[:, S:]


def _sel_attend(q, k, v, kpos, qpos, slopes):
    s = jnp.einsum('bqgrd,bqgkd->bqgrk', q, k).astype(jnp.float32) * (q.shape[-1] ** -0.5)
    rel = qpos[None, :, None, None] - kpos
    s = s - slopes[None, None, :, :, None] * rel[:, :, :, None, :].astype(jnp.float32)
    m = (rel >= 0)[:, :, :, None, :]
    p = jnp.where(m, jax.nn.softmax(jnp.where(m, s, NEG_INF), axis=-1), 0.0)
    return jnp.einsum('bqgrk,bqgkd->bqgrd', p.astype(v.dtype), v)


def _nsa_project(h, w_in):
    B, T = h.shape[:2]
    p = h @ w_in
    nq = NSA_HEADS * NSA_HD
    nkv = 2 * NSA_KV * NSA_HD
    q = p[..., :nq].reshape(B, T, NSA_KV, NSA_REP, NSA_HD)
    kv = p[..., nq:nq + 3 * nkv].reshape(B, T, 3, NSA_KV, 2, NSA_HD)
    gates = jax.nn.sigmoid(p[..., nq + 3 * nkv:].reshape(B, T, 3, NSA_KV, NSA_REP))
    return q, kv[:, :, 0], kv[:, :, 1], kv[:, :, 2], gates


def _nsa_compress(blocks, pe, w1, w2):
    x = blocks + pe[:, None]
    hid = jax.nn.silu(jnp.einsum('...nlgcd,cldh->...ngch', x, w1))
    return jnp.einsum('...ngch,chd->...ngcd', hid, w2)


def _nsa_cmp_attend(q, cmp, qpos, slopes):
    nb = cmp.shape[-4]
    j = jnp.arange(nb)
    mask = j[None, :] < (qpos // NSA_BLOCK)[:, None]
    dist = (qpos[:, None] - (j[None, :] * NSA_BLOCK + NSA_BLOCK - 1)).astype(jnp.float32)
    o, p = _attend(q, cmp[..., 0, :], cmp[..., 1, :], mask, dist, slopes)
    return o, p.sum(axis=-2)


def _nsa_merge(gates, o_c, o_s, o_w, w_out):
    o = gates[:, :, 0, :, :, None] * o_c + gates[:, :, 1, :, :, None] * o_s + gates[:, :, 2, :, :, None] * o_w
    return o.reshape(o.shape[0], o.shape[1], -1) @ w_out


def _nsa_layer(hp, hs, cache_cmp, cache_slc, win_all, page_table, li, w_in, pe, w1, w2, w_out):
    L = NSA_BLOCK
    B, T = hp.shape[:2]
    nq = NSA_HEADS * NSA_HD
    nkv = 2 * NSA_KV * NSA_HD
    kv_shape = (NSA_KV, 2, NSA_HD)
    pe2, w1bd, w2bd = _compress_weights(pe, w1, w2)
    pp = hp @ w_in
    kvc_rows = pp[..., nq:nq + nkv]
    kvc = kvc_rows.reshape((B, T) + kv_shape)
    kvs = pp[..., nq + nkv:nq + 2 * nkv].reshape((B, T) + kv_shape)
    win_p = pp[:, T - min(NSA_WINDOW, T):, nq + 2 * nkv:nq + 3 * nkv].reshape((B, min(NSA_WINDOW, T)) + kv_shape)
    cmp_p = _nsa_compress_dense(kvc_rows.reshape(B * T * N_SLABS, NSA_HD), pe2, w1bd, w2bd).reshape(B, T // L, nkv)
    gl = pp[..., nq + 3 * nkv:].reshape(B, T, 3, NSA_KV, NSA_REP).transpose(0, 3, 1, 2, 4)
    o_p = _nsa_prompt_attention(pp, cmp_p, gl.reshape(B, NSA_KV, T, 3 * NSA_REP))
    yp = o_p @ w_out

    Bd, S = hs.shape[:2]
    G, R, HD = NSA_KV, NSA_REP, NSA_HD
    ps = hs @ w_in
    kvc_s = ps[..., nq:nq + nkv].reshape((Bd, S) + kv_shape)
    kvs_s = ps[..., nq + nkv:nq + 2 * nkv].reshape((Bd, S) + kv_shape)
    kvw_s = ps[..., nq + 2 * nkv:nq + 3 * nkv].reshape((Bd, S) + kv_shape)
    q_rows = ps[..., :nq].reshape(Bd, S, G, R, HD).transpose(0, 2, 1, 3, 4).reshape(Bd, G, S * R, HD)
    gate_rows = ps[..., nq + 3 * nkv:].reshape(Bd, S, 3, G, R).transpose(0, 3, 1, 4, 2).reshape(Bd, G, S * R, 3)
    new_kv = ps[..., nq + nkv:nq + 3 * nkv].reshape(Bd, S, 2, G, 2, HD).transpose(0, 3, 2, 4, 1, 5)
    new_kv = jnp.pad(new_kv, [(0, 0)] * 4 + [(0, NEW_ROWS - S), (0, 0)])
    cache_rows = cache_cmp.reshape(cache_cmp.shape[:2] + (PAGE_SIZE * N_SLABS, HD))
    cmp = _nsa_compress_paged(cache_rows, li, page_table.reshape(-1), pe2, w1bd, w2bd)
    o_c, sel_out = _nsa_sample_cmp(q_rows, cmp)
    sel = sel_out[:, :, :S, :NSA_SEL_PAST].transpose(0, 2, 1, 3)
    bpp = PAGE_SIZE // L
    phys = page_table[jnp.arange(Bd)[:, None, None, None], sel // bpp] * bpp + sel % bpp
    cache_blocks = cache_slc.reshape(cache_slc.shape[0], cache_slc.shape[1] * bpp, SLAB_ROWS, HD)
    win_rows = win_all.reshape(win_all.shape[:2] + (win_all.shape[2] * N_SLABS, HD))
    o_s = _nsa_sample_sel(phys.reshape(-1), sel.reshape(-1), q_rows, o_c, gate_rows, new_kv, win_rows,
                          cache_blocks, li)
    ys = o_s.reshape(Bd, G, S, R, HD).transpose(0, 2, 1, 3, 4).reshape(Bd, S, nq) @ w_out
    win_s = jnp.concatenate([win_all[li], kvw_s], axis=1)[:, S:]
    return yp, ys, (kvc, kvc_s, kvs, kvs_s, win_p, win_s)


def _swa_project(h, w_in, b_in):
    B, T = h.shape[:2]
    p = h @ w_in + b_in
    nq = SWA_HEADS * SWA_HD
    q = p[..., :nq].reshape(B, T, SWA_KV, SWA_REP, SWA_HD)
    kv = p[..., nq:].reshape(B, T, SWA_KV, 2, SWA_HD)
    return q, kv


def _swa_layer(hp, hs, buf, w_in, b_in, sinks, w_out, b_out, slopes):
    sink = sinks.reshape(SWA_KV, SWA_REP)
    B, T = hp.shape[:2]
    q, kv = _swa_project(hp, w_in, b_in)
    o = _window_prompt(q, kv, SWA_WINDOW, slopes, sink)
    yp = o.reshape(B, T, -1) @ w_out + b_out
    buf_p = kv[:, T - min(SWA_WINDOW, T):]
    Bd, S = hs.shape[:2]
    q, kv_s = _swa_project(hs, w_in, b_in)
    o, buf_s = _window_sample(q, kv_s, buf, SWA_WINDOW, slopes, sink)
    ys = o.reshape(Bd, S, -1) @ w_out + b_out
    return yp, ys, (buf_p, buf_s)


def _gla_project(h, w_in, w_a2, b_a):
    B, T = h.shape[:2]
    p = h @ w_in
    nk = GLA_HEADS * GLA_DK
    nv = GLA_HEADS * GLA_DV
    q = p[..., :nk].reshape(B, T, GLA_HEADS, GLA_DK).astype(jnp.float32) * (GLA_DK ** -0.5)
    k = p[..., nk:2 * nk].reshape(B, T, GLA_HEADS, GLA_DK).astype(jnp.float32)
    v = p[..., 2 * nk:2 * nk + nv].reshape(B, T, GLA_HEADS, GLA_DV).astype(jnp.float32)
    r = p[..., 2 * nk + nv:2 * nk + 2 * nv].reshape(B, T, GLA_HEADS, GLA_DV)
    a = (p[..., 2 * nk + 2 * nv:] @ w_a2 + b_a).astype(jnp.float32)
    g = (jax.nn.log_sigmoid(a) / GLA_NORMALIZER).reshape(B, T, GLA_HEADS, GLA_DK)
    return q, k, v, g, r


def _gla_chunk(state, q, k, v, g):
    C = q.shape[1]
    b = jnp.cumsum(g, axis=1)
    causal = jnp.tril(jnp.ones((C, C), dtype=bool))
    o_inter = jnp.einsum('bthk,bhkv->bthv', q * jnp.exp(b), state)
    diff = b[:, :, None] - b[:, None, :]
    decay = jnp.exp(jnp.where(causal[None, :, :, None, None], diff, -jnp.inf))
    a = jnp.einsum('bthk,btshk,bshk->btsh', q, decay, k)
    o_intra = jnp.einsum('btsh,bshv->bthv', a, v)
    b_last = b[:, -1]
    new_state = jnp.exp(b_last)[..., None] * state + jnp.einsum('bshk,bshv->bhkv', k * jnp.exp(b_last[:, None] - b), v)
    return new_state, o_inter + o_intra


def _gla_layer(hp, hs, state, w_in, w_a2, b_a, norm, w_out):
    def readout(o, r, h):
        y = _rmsnorm(o, norm).astype(h.dtype) * jax.nn.silu(r)
        return y.reshape(h.shape[0], h.shape[1], -1) @ w_out

    B, T = hp.shape[:2]
    q, k, v, g, r = _gla_project(hp, w_in, w_a2, b_a)
    nc = T // GLA_CHUNK
    chunks = lambda a: a.reshape((B, nc, GLA_CHUNK) + a.shape[2:]).swapaxes(0, 1)
    s0 = jnp.zeros((B, GLA_HEADS, GLA_DK, GLA_DV), jnp.float32)
    s_p, o = lax.scan(lambda s, xs: _gla_chunk(s, *xs), s0, (chunks(q), chunks(k), chunks(v), chunks(g)))
    o = o.swapaxes(0, 1).reshape(B, T, GLA_HEADS, GLA_DV)
    yp = readout(o, r, hp)
    q, k, v, g, r = _gla_project(hs, w_in, w_a2, b_a)
    s_s, o = _gla_chunk(state.astype(jnp.float32), q, k, v, g)
    ys = readout(o, r, hs)
    return yp, ys, (s_p.astype(hp.dtype), s_s.astype(state.dtype))


def kernel(x_prompt, x_sample, cache_nsa_cmp, cache_nsa_slc, state_nsa_win, state_swa_kv, state_gla,
           page_table, c_prompt, c_sample, ada_w, ada_b, norm_mix, norm_ffn, norm_final,
           nsa_w_in, nsa_cmp_pe, nsa_cmp_w1, nsa_cmp_w2, nsa_w_out,
           swa_w_in, swa_b_in, swa_sinks, swa_w_out, swa_b_out,
           gla_w_in, gla_w_a2, gla_b_a, gla_norm, gla_w_out,
           moe_w_router, moe_b_router, moe_w1, moe_b1, moe_w2, moe_b2):
    B, T = x_prompt.shape[:2]
    Bd, S = x_sample.shape[:2]
    n_p = B * T
    nsa_slopes = _alibi_slopes(NSA_HEADS, NSA_KV)
    swa_slopes = _alibi_slopes(SWA_HEADS, SWA_KV)
    xp, xs = x_prompt, x_sample
    nsa_new, swa_new, gla_new = [], [], []
    for i in range(DEPTH):
        mp = _adaln(c_prompt, ada_w[i], ada_b[i])
        ms = _adaln(c_sample, ada_w[i], ada_b[i])
        hp = _rmsnorm(xp, norm_mix[i]) * (1.0 + mp[:, 1]) + mp[:, 0]
        hs = _rmsnorm(xs, norm_mix[i]) * (1.0 + ms[:, 1]) + ms[:, 0]
        kind, li = i % N_MIXERS, i // N_MIXERS
        if kind == 0:
            yp, ys, st = _nsa_layer(hp, hs, cache_nsa_cmp, cache_nsa_slc, state_nsa_win, page_table, li,
                                    nsa_w_in[li], nsa_cmp_pe[li], nsa_cmp_w1[li], nsa_cmp_w2[li], nsa_w_out[li])
            nsa_new.append(st)
        elif kind == 1:
            yp, ys, st = _swa_layer(hp, hs, state_swa_kv[li], swa_w_in[li], swa_b_in[li], swa_sinks[li],
                                    swa_w_out[li], swa_b_out[li], swa_slopes)
            swa_new.append(st)
        else:
            yp, ys, st = _gla_layer(hp, hs, state_gla[li], gla_w_in[li], gla_w_a2[li], gla_b_a[li],
                                    gla_norm[li], gla_w_out[li])
            gla_new.append(st)
        xp = xp + mp[:, 2] * yp
        xs = xs + ms[:, 2] * ys
        hp = _rmsnorm(xp, norm_ffn[i]) * (1.0 + mp[:, 4]) + mp[:, 3]
        hs = _rmsnorm(xs, norm_ffn[i]) * (1.0 + ms[:, 4]) + ms[:, 3]
        h_all = jnp.concatenate([hp.reshape(n_p, D_MODEL), hs.reshape(Bd * S, D_MODEL)], axis=0)
        y_all = _moe(h_all, moe_w_router[i], moe_b_router[i], i, moe_w1, moe_b1, moe_w2, moe_b2)
        xp = xp + mp[:, 5] * y_all[:n_p].reshape(B, T, D_MODEL)
        xs = xs + ms[:, 5] * y_all[n_p:].reshape(Bd, S, D_MODEL)
    y_prompt = _rmsnorm(xp, norm_final)
    y_sample = _rmsnorm(xs, norm_final)
    stack = lambda sts, k: jnp.stack([st[k] for st in sts])
    return (y_prompt, y_sample, stack(nsa_new, 0), stack(nsa_new, 1), stack(nsa_new, 2), stack(nsa_new, 3),
            stack(nsa_new, 4), stack(nsa_new, 5), stack(swa_new, 0), stack(swa_new, 1),
            stack(gla_new, 0), stack(gla_new, 1))
```

```python
import functools

import jax
import jax.numpy as jnp
from jax import lax
from jax.experimental import pallas as pl
from jax.experimental.pallas import tpu as pltpu

D_MODEL = 2048
DEPTH = 4
PAST_LEN = 16384
PAGE_SIZE = 128
N_MIXERS = 3

NSA_HEADS = 16
NSA_KV = 2
NSA_HD = D_MODEL // NSA_HEADS
NSA_REP = NSA_HEADS // NSA_KV
NSA_BLOCK = 64
NSA_TOPN = 16
NSA_WINDOW = 512
NSA_SEL_QB = 64
SWA_HEADS = 32
SWA_KV = 4
SWA_HD = D_MODEL // SWA_HEADS
SWA_REP = SWA_HEADS // SWA_KV
SWA_WINDOW = 128
GLA_HEADS = 4
GLA_DK = D_MODEL // 2 // GLA_HEADS
GLA_DV = D_MODEL // GLA_HEADS
GLA_NORMALIZER = 16.0
GLA_CHUNK = 64
N_EXPERTS = 32
TOP_K = 4
D_FF = D_MODEL
SWIGLU_LIMIT = 7.0
SWIGLU_ALPHA = 1.702
QBLOCK = 128
NEG_INF = -1e30
EPS = 1e-6

VMEM_LIMIT_BYTES = 56 * 1024 * 1024
MOE_TM = 256
MOE_TF = 1024
MOE_TN = 1024
CAST_ROWS = 256


def _cast_rows(src_ref, dst_ref):
    n = dst_ref.shape[0] // CAST_ROWS

    def body(c, carry):
        r = pl.multiple_of(c * CAST_ROWS, CAST_ROWS)
        dst_ref[pl.ds(r, CAST_ROWS), :] = src_ref[pl.ds(r, CAST_ROWS), :].astype(jnp.bfloat16)
        return carry

    lax.fori_loop(0, n, body, 0)


def _stream_slab(j, i, n_j, meta, slab_copies, convert):
    te_ref, tf_ref, _, nx_ref, lg_ref, gi_ref, ng_ref = meta

    @pl.when(tf_ref[i] == 1)
    def _():
        slot = lax.rem(j * ng_ref[0] + gi_ref[i], 2)

        @pl.when((j == 0) & (i == 0))
        def _():
            for cp in slab_copies(te_ref[0], 0, slot):
                cp.start()

        for cp in slab_copies(te_ref[i], j, slot):
            cp.wait()
        convert(slot)
        nxt_j = j + lg_ref[i]

        @pl.when(nxt_j < n_j)
        def _():
            for cp in slab_copies(nx_ref[i], nxt_j, 1 - slot):
                cp.start()


def _moe_up_kernel(*refs, layer):
    meta, (x_ref, w1_ref, bg_ref, bu_ref, h_ref, wbuf, wg_s, wu_s, sem) = refs[:7], refs[7:]
    tv_ref = meta[2]
    j = pl.program_id(0)
    i = pl.program_id(1)

    def slab_copies(e, jj, slot):
        col = pl.multiple_of(jj * MOE_TF, MOE_TF)
        return [pltpu.make_async_copy(w1_ref.at[layer, e, :, pl.ds(c0 + col, MOE_TF)], wbuf.at[slot, c], sem.at[slot])
                for c, c0 in enumerate((0, D_FF))]

    def convert(slot):
        _cast_rows(wbuf.at[slot, 0], wg_s)
        _cast_rows(wbuf.at[slot, 1], wu_s)

    _stream_slab(j, i, pl.num_programs(0), meta, slab_copies, convert)

    @pl.when(tv_ref[i] == 1)
    def _():
        x = x_ref[...].astype(jnp.bfloat16)
        g = jnp.dot(x, wg_s[...], preferred_element_type=jnp.float32) + bg_ref[...]
        u = jnp.dot(x, wu_s[...], preferred_element_type=jnp.float32) + bu_ref[...]
        g = jnp.minimum(g, SWIGLU_LIMIT)
        u = jnp.clip(u, -SWIGLU_LIMIT, SWIGLU_LIMIT)
        sig = 1.0 / (1.0 + jnp.exp(-SWIGLU_ALPHA * g))
        h_ref[...] = (g * sig * (u + 1.0)).astype(h_ref.dtype)

    @pl.when(tv_ref[i] == 0)
    def _():
        h_ref[...] = jnp.zeros_like(h_ref)


def _moe_down_kernel(*refs, layer):
    meta, (h_ref, w2_ref, b_ref, y_ref, wbuf, w_s, sem) = refs[:7], refs[7:]
    tv_ref = meta[2]
    j = pl.program_id(0)
    i = pl.program_id(1)

    def slab_copies(e, jj, slot):
        col = pl.multiple_of(jj * MOE_TN, MOE_TN)
        return [pltpu.make_async_copy(w2_ref.at[layer, e, :, pl.ds(col, MOE_TN)], wbuf.at[slot], sem.at[slot])]

    _stream_slab(j, i, pl.num_programs(0), meta, slab_copies, lambda slot: _cast_rows(wbuf.at[slot], w_s))

    @pl.when(tv_ref[i] == 1)
    def _():
        y_ref[...] = jnp.dot(h_ref[...], w_s[...], preferred_element_type=jnp.float32) + b_ref[...]

    @pl.when(tv_ref[i] == 0)
    def _():
        y_ref[...] = jnp.zeros_like(y_ref)


def _moe_ffn(x_sorted, meta, layer, w1, b1, w2, b2):
    P, D = x_sorted.shape
    n_tiles = P // MOE_TM
    nj1 = D_FF // MOE_TF
    b1r = b1.reshape(DEPTH, N_EXPERTS, 1, 2 * D_FF)
    b2r = b2.reshape(DEPTH, N_EXPERTS, 1, D)
    params = pltpu.CompilerParams(dimension_semantics=("arbitrary", "arbitrary"),
                                  vmem_limit_bytes=VMEM_LIMIT_BYTES)
    bf16 = jnp.bfloat16
    h = pl.pallas_call(
        functools.partial(_moe_up_kernel, layer=layer),
        grid_spec=pltpu.PrefetchScalarGridSpec(
            num_scalar_prefetch=7,
            grid=(nj1, n_tiles),
            in_specs=[
                pl.BlockSpec((MOE_TM, D), lambda j, i, *m: (i, 0)),
                pl.BlockSpec(memory_space=pl.ANY),
                pl.BlockSpec((None, None, 1, MOE_TF), lambda j, i, te, *m: (layer, te[i], 0, j)),
                pl.BlockSpec((None, None, 1, MOE_TF), lambda j, i, te, *m: (layer, te[i], 0, nj1 + j)),
            ],
            out_specs=pl.BlockSpec((MOE_TM, MOE_TF), lambda j, i, *m: (i, j)),
            scratch_shapes=[pltpu.VMEM((2, 2, D, MOE_TF), jnp.float32),
                            pltpu.VMEM((D, MOE_TF), bf16), pltpu.VMEM((D, MOE_TF), bf16),
                            pltpu.SemaphoreType.DMA((2,))],
        ),
        out_shape=jax.ShapeDtypeStruct((P, D_FF), bf16),
        compiler_params=params,
        name="moe_up",
    )(*meta, x_sorted, w1, b1r, b1r)
    y = pl.pallas_call(
        functools.partial(_moe_down_kernel, layer=layer),
        grid_spec=pltpu.PrefetchScalarGridSpec(
            num_scalar_prefetch=7,
            grid=(D // MOE_TN, n_tiles),
            in_specs=[
                pl.BlockSpec((MOE_TM, D_FF), lambda j, i, *m: (i, 0)),
                pl.BlockSpec(memory_space=pl.ANY),
                pl.BlockSpec((None, None, 1, MOE_TN), lambda j, i, te, *m: (layer, te[i], 0, j)),
            ],
            out_specs=pl.BlockSpec((MOE_TM, MOE_TN), lambda j, i, *m: (i, j)),
            scratch_shapes=[pltpu.VMEM((2, D_FF, MOE_TN), jnp.float32), pltpu.VMEM((D_FF, MOE_TN), bf16),
                            pltpu.SemaphoreType.DMA((2,))],
        ),
        out_shape=jax.ShapeDtypeStruct((P, D), jnp.float32),
        compiler_params=params,
        name="moe_down",
    )(*meta, h, w2, b2r)
    return y


DISPATCH_TOKENS = 320
COMBINE_TOKENS = 64


def _row_copy(src, src_row, dst, dst_row, sem):
    return pltpu.make_async_copy(src.at[pl.ds(src_row, 1), :], dst.at[pl.ds(dst_row, 1), :], sem)


def _moe_dispatch_kernel(dest_ref, x_ref, init_ref, out_ref, sem):
    del init_ref
    i = pl.program_id(0)

    def start(t, carry):
        tok = i * DISPATCH_TOKENS + t
        for k in range(TOP_K):
            _row_copy(x_ref, tok, out_ref, dest_ref[tok * TOP_K + k], sem).start()
        return carry

    def wait(t, carry):
        for k in range(TOP_K):
            _row_copy(x_ref, 0, out_ref, 0, sem).wait()
        return carry

    lax.fori_loop(0, DISPATCH_TOKENS, start, 0)
    lax.fori_loop(0, DISPATCH_TOKENS, wait, 0)


def _moe_dispatch(x, dest, n_rows):
    N, D = x.shape
    return pl.pallas_call(
        _moe_dispatch_kernel,
        grid_spec=pltpu.PrefetchScalarGridSpec(
            num_scalar_prefetch=1,
            grid=(N // DISPATCH_TOKENS,),
            in_specs=[pl.BlockSpec(memory_space=pl.ANY), pl.BlockSpec(memory_space=pl.ANY)],
            out_specs=pl.BlockSpec(memory_space=pl.ANY),
            scratch_shapes=[pltpu.SemaphoreType.DMA(())],
        ),
        out_shape=jax.ShapeDtypeStruct((n_rows, D), x.dtype),
        input_output_aliases={2: 0},
        compiler_params=pltpu.CompilerParams(dimension_semantics=("arbitrary",)),
        name="moe_dispatch",
    )(dest, x, jnp.zeros((n_rows, D), x.dtype))


def _moe_combine_kernel(dest_ref, y_ref, gate_ref, out_ref, buf, sem):
    i = pl.program_id(0)
    n_steps = pl.num_programs(0)
    slot = lax.rem(i, 2)
    CT = COMBINE_TOKENS

    def for_rows(step, slot_, act):
        def body(t, carry):
            for k in range(TOP_K):
                a = (step * CT + t) * TOP_K + k
                act(_row_copy(y_ref, dest_ref[a], buf.at[slot_], k * CT + t, sem.at[slot_]))
            return carry
        lax.fori_loop(0, CT, body, 0)

    @pl.when(i == 0)
    def _():
        for_rows(0, 0, lambda cp: cp.start())

    @pl.when(i + 1 < n_steps)
    def _():
        for_rows(i + 1, 1 - slot, lambda cp: cp.start())

    for_rows(i, slot, lambda cp: cp.wait())
    gate = gate_ref[...]
    acc = buf[slot, 0:CT, :] * gate[:, 0:1]
    for k in range(1, TOP_K):
        acc = acc + buf[slot, k * CT:(k + 1) * CT, :] * gate[:, k:k + 1]
    out_ref[...] = acc


def _moe_combine(y, dest, gate):
    D = y.shape[1]
    N = gate.shape[0]
    CT = COMBINE_TOKENS
    return pl.pallas_call(
        _moe_combine_kernel,
        grid_spec=pltpu.PrefetchScalarGridSpec(
            num_scalar_prefetch=1,
            grid=(N // CT,),
            in_specs=[pl.BlockSpec(memory_space=pl.ANY), pl.BlockSpec((CT, TOP_K), lambda i, d: (i, 0))],
            out_specs=pl.BlockSpec((CT, D), lambda i, d: (i, 0)),
            scratch_shapes=[pltpu.VMEM((2, CT * TOP_K, D), jnp.float32), pltpu.SemaphoreType.DMA((2,))],
        ),
        out_shape=jax.ShapeDtypeStruct((N, D), jnp.float32),
        compiler_params=pltpu.CompilerParams(dimension_semantics=("arbitrary",),
                                             vmem_limit_bytes=VMEM_LIMIT_BYTES),
        name="moe_combine",
    )(dest, y, gate)


def _moe(x, w_r, b_r, layer, w1, b1, w2, b2):
    N, D = x.shape
    logits = (x @ w_r + b_r).astype(jnp.float32)
    top_v, top_i = lax.top_k(logits, TOP_K)
    gate = jax.nn.softmax(top_v, axis=-1)
    A = N * TOP_K
    e_flat = top_i.reshape(-1).astype(jnp.int32)
    onehot = (e_flat[:, None] == jnp.arange(N_EXPERTS, dtype=jnp.int32)[None, :]).astype(jnp.int32)
    csum = jnp.cumsum(onehot, axis=0)
    counts = csum[-1]
    rank = jnp.take_along_axis(csum, e_flat[:, None], axis=1)[:, 0] - 1
    padded = (counts + MOE_TM - 1) // MOE_TM * MOE_TM
    pend = jnp.cumsum(padded)
    pstart = pend - padded
    dest = pstart[e_flat] + rank
    n_tiles = -(-(A + N_EXPERTS * (MOE_TM - 1)) // MOE_TM)
    P = n_tiles * MOE_TM
    tile_start = jnp.arange(n_tiles, dtype=jnp.int32) * MOE_TM
    tile_valid = tile_start < pend[-1]
    tile_e = jnp.minimum(jnp.searchsorted(pend, tile_start, side='right'), N_EXPERTS - 1).astype(jnp.int32)
    prev_e = jnp.concatenate([jnp.full((1,), -1, jnp.int32), tile_e[:-1]])
    tile_first = ((tile_e != prev_e) & tile_valid).astype(jnp.int32)
    group_idx = jnp.cumsum(tile_first) - 1
    n_groups = jnp.sum(tile_first)
    group_e = jnp.argsort(counts == 0, stable=True).astype(jnp.int32)
    next_e = group_e[(group_idx + 1) % n_groups]
    last_group = (group_idx + 1 == n_groups).astype(jnp.int32)
    meta = (tile_e, tile_first, tile_valid.astype(jnp.int32), next_e, last_group, group_idx.astype(jnp.int32),
            n_groups.reshape(1).astype(jnp.int32))
    x_sorted = _moe_dispatch(x, dest, P)
    y = _moe_ffn(x_sorted, meta, layer, w1, b1, w2, b2)
    return _moe_combine(y, dest, gate)


def _dot_nt(a, b):
    return lax.dot_general(a, b, (((1,), (1,)), ((), ())), preferred_element_type=jnp.float32)


N_SLABS = 2 * NSA_KV
SLAB_ROWS = NSA_BLOCK * N_SLABS
CMP_DENSE_BLOCKS = 64
CMP_PAGES = 64
CMP_UNROLL = 8


def _compress_rows(load_row, n_blocks, pe_ref, w1_ref, w2_ref, out_ref):
    W = 2 * NSA_HD
    for g in range(NSA_KV):
        def step(l, acc):
            x = jnp.concatenate([load_row(l, 2 * g), load_row(l, 2 * g + 1)], axis=1)
            x = (x + pe_ref[pl.ds(l, 1), :]).astype(jnp.bfloat16)
            return acc + jnp.dot(x, w1_ref[l], preferred_element_type=jnp.float32)

        hid = lax.fori_loop(0, NSA_BLOCK, step, jnp.zeros((n_blocks, W), jnp.float32), unroll=CMP_UNROLL)
        hid = hid * (1.0 / (1.0 + jnp.exp(-hid)))
        out_ref[:, g * W:(g + 1) * W] = jnp.dot(hid.astype(jnp.bfloat16), w2_ref[...],
                                                preferred_element_type=jnp.float32)


def _compress_weights(pe, w1, w2):
    z1 = jnp.zeros_like(w1[0])
    w1bd = jnp.concatenate([jnp.concatenate([w1[0], z1], axis=2), jnp.concatenate([z1, w1[1]], axis=2)], axis=1)
    z2 = jnp.zeros_like(w2[0])
    w2bd = jnp.concatenate([jnp.concatenate([w2[0], z2], axis=1), jnp.concatenate([z2, w2[1]], axis=1)], axis=0)
    return pe.reshape(NSA_BLOCK, 2 * NSA_HD), w1bd.astype(jnp.bfloat16), w2bd.astype(jnp.bfloat16)


def _compress_dense_kernel(x_ref, pe_ref, w1_ref, w2_ref, out_ref):
    n_blocks = out_ref.shape[0]
    load = lambda l, j: x_ref[pl.ds(l * N_SLABS + j, n_blocks, stride=SLAB_ROWS), :]
    _compress_rows(load, n_blocks, pe_ref, w1_ref, w2_ref, out_ref)


def _nsa_compress_dense(kv_slab_rows, pe2, w1bd, w2bd):
    nblk = kv_slab_rows.shape[0] // SLAB_ROWS
    step = min(nblk, CMP_DENSE_BLOCKS)
    W4 = N_SLABS * NSA_HD
    const = lambda shape: pl.BlockSpec(shape, lambda i: (0,) * len(shape))
    return pl.pallas_call(
        _compress_dense_kernel,
        grid=(nblk // step,),
        in_specs=[pl.BlockSpec((step * SLAB_ROWS, NSA_HD), lambda i: (i, 0)),
                  const(pe2.shape), const(w1bd.shape), const(w2bd.shape)],
        out_specs=pl.BlockSpec((step, W4), lambda i: (i, 0)),
        out_shape=jax.ShapeDtypeStruct((nblk, W4), jnp.float32),
        compiler_params=pltpu.CompilerParams(dimension_semantics=("arbitrary",),
                                             vmem_limit_bytes=VMEM_LIMIT_BYTES),
        name="nsa_compress_dense",
    )(kv_slab_rows, pe2, w1bd, w2bd)


def _page_copy(pid_ref, cache_ref, buf, sem, layer, step, slot, p):
    rows = cache_ref.shape[2]
    pid = pid_ref[step * CMP_PAGES + p]
    return pltpu.make_async_copy(cache_ref.at[layer, pid], buf.at[slot, pl.ds(p * rows, rows), :], sem.at[slot])


def _compress_paged_kernel(pid_ref, cache_ref, pe_ref, w1_ref, w2_ref, out_ref, buf, sem, *, layer):
    i = pl.program_id(0)
    n_steps = pl.num_programs(0)
    slot = lax.rem(i, 2)

    def for_pages(step, slot_, act):
        def body(p, carry):
            act(_page_copy(pid_ref, cache_ref, buf, sem, layer, step, slot_, p))
            return carry
        lax.fori_loop(0, CMP_PAGES, body, 0)

    @pl.when(i == 0)
    def _():
        for_pages(0, 0, lambda cp: cp.start())

    @pl.when(i + 1 < n_steps)
    def _():
        for_pages(i + 1, 1 - slot, lambda cp: cp.start())

    for_pages(i, slot, lambda cp: cp.wait())
    n_blocks = out_ref.shape[0]
    load = lambda l, j: buf[slot, pl.ds(l * N_SLABS + j, n_blocks, stride=SLAB_ROWS), :]
    _compress_rows(load, n_blocks, pe_ref, w1_ref, w2_ref, out_ref)


def _nsa_compress_paged(cache, layer, page_ids, pe2, w1bd, w2bd):
    page_rows = cache.shape[2]
    n_pages = page_ids.shape[0]
    bpp = page_rows // SLAB_ROWS
    W4 = N_SLABS * NSA_HD
    const = lambda shape: pl.BlockSpec(shape, lambda i, pid: (0,) * len(shape))
    return pl.pallas_call(
        functools.partial(_compress_paged_kernel, layer=layer),
        grid_spec=pltpu.PrefetchScalarGridSpec(
            num_scalar_prefetch=1,
            grid=(n_pages // CMP_PAGES,),
            in_specs=[pl.BlockSpec(memory_space=pl.ANY), const(pe2.shape), const(w1bd.shape), const(w2bd.shape)],
            out_specs=pl.BlockSpec((CMP_PAGES * bpp, W4), lambda i, pid: (i, 0)),
            scratch_shapes=[pltpu.VMEM((2, CMP_PAGES * page_rows, NSA_HD), jnp.float32),
                            pltpu.SemaphoreType.DMA((2,))],
        ),
        out_shape=jax.ShapeDtypeStruct((n_pages * bpp, W4), jnp.float32),
        compiler_params=pltpu.CompilerParams(dimension_semantics=("arbitrary",),
                                             vmem_limit_bytes=VMEM_LIMIT_BYTES),
        name="nsa_compress_paged",
    )(page_ids, cache, pe2, w1bd, w2bd)


NSA_TQ = 128
NSA_TK = 512
LOG2E = 1.4426950408889634
MASK_DIST = 1e30


def _alibi_slope(head_index, n_heads):
    return 2.0 ** (-8.0 * (head_index + 1) / n_heads)


def _nsa_prompt_kernel(q_ref, kc_ref, vc_ref, ks_ref, vs_ref, kw_ref, vw_ref, gl_ref, o_ref,
                       q_s, kc_s, vc_s, ks_s, vs_s, kw_s, vw_s, s_s, p_s, acc_s, m_s, l_s, a_s):
    g = pl.program_id(1)
    qi = pl.program_id(2)
    TQ, HD, R, L = NSA_TQ, NSA_HD, NSA_REP, NSA_BLOCK
    T = ks_ref.shape[1]
    nb = kc_ref.shape[1]
    scale = HD ** -0.5
    bf16 = jnp.bfloat16

    @pl.when(qi == 0)
    def _():
        ks_s[...] = ks_ref[0].astype(bf16)
        vs_s[...] = vs_ref[0].astype(bf16)
        kw_s[...] = kw_ref[0].astype(bf16)
        vw_s[...] = vw_ref[0].astype(bf16)
        kc_s[...] = jnp.zeros_like(kc_s)
        vc_s[...] = jnp.zeros_like(vc_s)
        kc_s[0:nb, :] = kc_ref[0].astype(bf16)
        vc_s[0:nb, :] = vc_ref[0].astype(bf16)

    for r in range(R):
        q_s[r * TQ:(r + 1) * TQ, :] = q_ref[0, :, r * HD:(r + 1) * HD].astype(bf16)

    gate = 1.0 / (1.0 + jnp.exp(-gl_ref[0, 0]))
    slopes = [jnp.where(g == 0, _alibi_slope(r, NSA_HEADS), _alibi_slope(R + r, NSA_HEADS)) for r in range(R)]
    qpos0 = qi * TQ

    row_c = lax.broadcasted_iota(jnp.int32, (TQ, 128), 0) + qpos0
    col_c = lax.broadcasted_iota(jnp.int32, (TQ, 128), 1)
    cur_c = lax.shift_right_logical(row_c, 6)
    mask_c = col_c < cur_c
    dist_c = (row_c - (col_c * L + (L - 1))).astype(jnp.float32)
    imp = jnp.zeros((TQ, 128), jnp.float32)
    s_s[:, 0:128] = _dot_nt(q_s[...], kc_s[...]) * scale
    for r in range(R):
        s = s_s[r * TQ:(r + 1) * TQ, 0:128] - slopes[r] * dist_c
        s = jnp.where(mask_c, s, NEG_INF)
        p = jnp.exp(s - jnp.max(s, axis=-1, keepdims=True))
        p = p / jnp.sum(p, axis=-1, keepdims=True)
        p = jnp.where(mask_c, p, 0.0)
        imp = imp + p
        o = jnp.dot(p.astype(bf16), vc_s[...], preferred_element_type=jnp.float32)
        o_ref[0, :, r * HD:(r + 1) * HD] = gate[:, r:r + 1] * o

    valid = col_c <= cur_c
    forced = valid & ((col_c == 0) | (col_c >= cur_c - 1))
    score = jnp.where(forced, jnp.inf, jnp.where(valid, imp, -jnp.inf))
    sc_t = jnp.transpose(score)[0:32, :]
    jrow = lax.broadcasted_iota(jnp.int32, (32, TQ), 0)
    rank = jnp.zeros((32, TQ), jnp.int32)
    for i in range(32):
        row = sc_t[i:i + 1, :]
        ahead = (row > sc_t) | ((row == sc_t) & (jrow > i))
        rank = rank + ahead.astype(jnp.int32)
    sel_t = jnp.where(rank < NSA_TOPN, 1.0, 0.0)
    sel_t = jnp.concatenate([sel_t, jnp.zeros((128 - 32, TQ), jnp.float32)], axis=0)
    sel = jnp.transpose(sel_t).astype(bf16)

    WK = NSA_WINDOW + TQ
    w0 = pl.multiple_of(jnp.maximum(qi - NSA_WINDOW // TQ, 0) * TQ, TQ)
    row_w = lax.broadcasted_iota(jnp.int32, (TQ, WK), 0) + qpos0
    rel_w = row_w - (lax.broadcasted_iota(jnp.int32, (TQ, WK), 1) + w0)
    ok_w = (rel_w >= 0) & (rel_w < NSA_WINDOW)
    relm_w = jnp.where(ok_w, rel_w.astype(jnp.float32), MASK_DIST)
    s_s[...] = _dot_nt(q_s[...], kw_s[pl.ds(w0, WK), :]) * (scale * LOG2E)
    for r in range(R):
        rows = slice(r * TQ, (r + 1) * TQ)
        s = s_s[rows, :] - (slopes[r] * LOG2E) * relm_w
        p = jnp.exp2(s - jnp.max(s, axis=-1, keepdims=True))
        l_s[rows, :] = jnp.sum(p, axis=-1, keepdims=True)
        p_s[rows, :] = p.astype(bf16)
    acc_s[...] = jnp.dot(p_s[...], vw_s[pl.ds(w0, WK), :], preferred_element_type=jnp.float32) / l_s[...]
    for r in range(R):
        rows = slice(r * TQ, (r + 1) * TQ)
        o_ref[0, :, r * HD:(r + 1) * HD] += gate[:, 2 * R + r:2 * R + r + 1] * acc_s[rows, :]

    m_s[...] = jnp.full_like(m_s, -jnp.inf)
    l_s[...] = jnp.zeros_like(l_s)
    acc_s[...] = jnp.zeros_like(acc_s)
    TK = NSA_TK
    n_chunks = (qpos0 + TQ + TK - 1) // TK

    def chunk(c, carry):
        k0 = pl.multiple_of(c * TK, TK)
        row_t = lax.broadcasted_iota(jnp.int32, (TQ, TK), 0) + qpos0
        rel = row_t - (lax.broadcasted_iota(jnp.int32, (TQ, TK), 1) + k0)
        blk = lax.shift_right_logical(lax.broadcasted_iota(jnp.int32, (128, TK), 1) + k0, 6)
        expand = jnp.where(lax.broadcasted_iota(jnp.int32, (128, TK), 0) == blk, 1.0, 0.0).astype(bf16)
        picked = jnp.dot(sel, expand, preferred_element_type=jnp.float32)
        ok = (picked > 0.5) & (rel >= 0)
        relm = jnp.where(ok, rel.astype(jnp.float32), MASK_DIST)
        s_s[:, 0:TK] = _dot_nt(q_s[...], ks_s[pl.ds(k0, TK), :]) * (scale * LOG2E)
        for r in range(R):
            rows = slice(r * TQ, (r + 1) * TQ)
            s = s_s[rows, 0:TK] - (slopes[r] * LOG2E) * relm
            m_old = m_s[rows, :]
            m_new = jnp.maximum(m_old, jnp.max(s, axis=-1, keepdims=True))
            alpha = jnp.exp2(m_old - m_new)
            p = jnp.exp2(s - m_new)
            l_s[rows, :] = alpha * l_s[rows, :] + jnp.sum(p, axis=-1, keepdims=True)
            p_s[rows, 0:TK] = p.astype(bf16)
            a_s[rows, :] = alpha
            m_s[rows, :] = m_new
        acc_s[...] = a_s[...] * acc_s[...] + jnp.dot(p_s[:, 0:TK], vs_s[pl.ds(k0, TK), :],
                                                     preferred_element_type=jnp.float32)
        return carry

    lax.fori_loop(0, n_chunks, chunk, 0)
    acc_s[...] = acc_s[...] / l_s[...]
    for r in range(R):
        rows = slice(r * TQ, (r + 1) * TQ)
        o_ref[0, :, r * HD:(r + 1) * HD] += gate[:, R + r:R + r + 1] * acc_s[rows, :]


def _nsa_prompt_attention(p, cmp, gate_logits):
    B, T, _ = p.shape
    nb = cmp.shape[1]
    HD, R, G, TQ = NSA_HD, NSA_REP, NSA_KV, NSA_TQ
    kv0 = NSA_HEADS
    lane_kv = lambda branch, c: (lambda b, g, i: (b, 0, kv0 + branch * 2 * G + g * 2 + c))
    bf16 = jnp.bfloat16
    return pl.pallas_call(
        _nsa_prompt_kernel,
        grid=(B, G, T // TQ),
        in_specs=[
            pl.BlockSpec((1, TQ, R * HD), lambda b, g, i: (b, i, g)),
            pl.BlockSpec((1, nb, HD), lambda b, g, i: (b, 0, 2 * g)),
            pl.BlockSpec((1, nb, HD), lambda b, g, i: (b, 0, 2 * g + 1)),
            pl.BlockSpec((1, T, HD), lane_kv(1, 0)),
            pl.BlockSpec((1, T, HD), lane_kv(1, 1)),
            pl.BlockSpec((1, T, HD), lane_kv(2, 0)),
            pl.BlockSpec((1, T, HD), lane_kv(2, 1)),
            pl.BlockSpec((1, 1, TQ, 3 * R), lambda b, g, i: (b, g, i, 0)),
        ],
        out_specs=pl.BlockSpec((1, TQ, R * HD), lambda b, g, i: (b, i, g)),
        out_shape=jax.ShapeDtypeStruct((B, T, NSA_HEADS * HD), jnp.float32),
        scratch_shapes=[
            pltpu.VMEM((R * TQ, HD), bf16),
            pltpu.VMEM((128, HD), bf16), pltpu.VMEM((128, HD), bf16),
            pltpu.VMEM((T, HD), bf16), pltpu.VMEM((T, HD), bf16),
            pltpu.VMEM((T, HD), bf16), pltpu.VMEM((T, HD), bf16),
            pltpu.VMEM((R * TQ, NSA_WINDOW + TQ), jnp.float32),
            pltpu.VMEM((R * TQ, NSA_WINDOW + TQ), bf16),
            pltpu.VMEM((R * TQ, HD), jnp.float32),
            pltpu.VMEM((R * TQ, 1), jnp.float32), pltpu.VMEM((R * TQ, 1), jnp.float32),
            pltpu.VMEM((R * TQ, 1), jnp.float32),
        ],
        compiler_params=pltpu.CompilerParams(dimension_semantics=("arbitrary", "arbitrary", "arbitrary"),
                                             vmem_limit_bytes=VMEM_LIMIT_BYTES),
        name="nsa_prompt_attention",
    )(p, cmp, cmp, p, p, p, p, gate_logits)


NSA_SEL_PAST = NSA_TOPN - 1
SEL_SLOTS = 1024
WIN_SLOTS = NSA_WINDOW + 128
NEW_ROWS = 16


def _row_slopes(g, n_rows):
    r = lax.broadcasted_iota(jnp.int32, (n_rows, 1), 0) % NSA_REP
    return jnp.exp2(-8.0 * (g * NSA_REP + r + 1).astype(jnp.float32) / NSA_HEADS)


def _nsa_sample_cmp_kernel(q_ref, kc_ref, vc_ref, oc_ref, sel_ref):
    g = pl.program_id(1)
    S = q_ref.shape[2] // NSA_REP
    nbp = kc_ref.shape[0]
    L = NSA_BLOCK
    n_rows = S * NSA_REP
    bf16 = jnp.bfloat16
    q = q_ref[0, 0].astype(bf16)
    row = lax.broadcasted_iota(jnp.int32, (n_rows, nbp), 0)
    col = lax.broadcasted_iota(jnp.int32, (n_rows, nbp), 1)
    qpos = PAST_LEN + row // NSA_REP
    cur = qpos // L
    mask = col < cur
    dist = (qpos - (col * L + (L - 1))).astype(jnp.float32)
    s = _dot_nt(q, kc_ref[...].astype(bf16)) * (NSA_HD ** -0.5) - _row_slopes(g, n_rows) * dist
    s = jnp.where(mask, s, NEG_INF)
    p = jnp.exp(s - jnp.max(s, axis=-1, keepdims=True))
    p = p / jnp.sum(p, axis=-1, keepdims=True)
    p = jnp.where(mask, p, 0.0)
    oc_ref[0, 0] = jnp.dot(p.astype(bf16), vc_ref[...].astype(bf16), preferred_element_type=jnp.float32)

    imp = jnp.sum(p.reshape(S, NSA_REP, nbp), axis=1)
    colS = lax.broadcasted_iota(jnp.int32, (S, nbp), 1)
    curS = (PAST_LEN + lax.broadcasted_iota(jnp.int32, (S, nbp), 0)) // L
    valid = colS < curS
    forced = valid & ((colS == 0) | (colS == curS - 1))
    score = jnp.where(forced, jnp.inf, jnp.where(valid, imp, -jnp.inf))
    score_t = jnp.transpose(jnp.concatenate([score, jnp.zeros((128 - S, nbp), jnp.float32)], axis=0))
    ii = lax.broadcasted_iota(jnp.int32, (nbp, nbp), 0)
    jj = lax.broadcasted_iota(jnp.int32, (nbp, nbp), 1)
    lane = lax.broadcasted_iota(jnp.int32, (1, 128), 1)
    jrow = lax.broadcasted_iota(jnp.int32, (1, nbp), 1).astype(jnp.float32)
    out_rows = []
    for si in range(S):
        c = score_t[:, si:si + 1]
        rw = score[si:si + 1, :]
        ahead = (c > rw) | ((c == rw) & (ii < jj))
        rank = jnp.sum(jnp.where(ahead, 1.0, 0.0), axis=0, keepdims=True)
        ids = jnp.zeros((1, 128), jnp.float32)
        for k in range(NSA_SEL_PAST):
            idx = jnp.sum(jnp.where(rank == k, jrow, 0.0), axis=1, keepdims=True)
            ids = jnp.where(lane == k, idx, ids)
        out_rows.append(ids.astype(jnp.int32))
    out_rows.append(jnp.zeros((8 - S, 128), jnp.int32))
    sel_ref[0, 0] = jnp.concatenate(out_rows, axis=0)


def _nsa_sample_cmp(q_rows, cmp_rows):
    Bd, G, n_rows, HD = q_rows.shape
    nbp = cmp_rows.shape[0] // Bd
    return pl.pallas_call(
        _nsa_sample_cmp_kernel,
        grid=(Bd, G),
        in_specs=[pl.BlockSpec((1, 1, n_rows, HD), lambda b, g: (b, g, 0, 0)),
                  pl.BlockSpec((nbp, HD), lambda b, g: (b, 2 * g)),
                  pl.BlockSpec((nbp, HD), lambda b, g: (b, 2 * g + 1))],
        out_specs=[pl.BlockSpec((1, 1, n_rows, HD), lambda b, g: (b, g, 0, 0)),
                   pl.BlockSpec((1, 1, 8, 128), lambda b, g: (b, g, 0, 0))],
        out_shape=[jax.ShapeDtypeStruct((Bd, G, n_rows, HD), jnp.float32),
                   jax.ShapeDtypeStruct((Bd, G, 8, 128), jnp.int32)],
        compiler_params=pltpu.CompilerParams(dimension_semantics=("arbitrary", "arbitrary"),
                                             vmem_limit_bytes=VMEM_LIMIT_BYTES),
        name="nsa_sample_cmp",
    )(q_rows, cmp_rows, cmp_rows)


def _nsa_sample_sel_kernel(phys_ref, sel_ref, q_ref, oc_ref, gate_ref, new_ref, win_ref, *rest):
    blk_refs = rest[:NSA_SEL_PAST]
    o_ref = rest[NSA_SEL_PAST]
    ks_s, vs_s, kw_s, vw_s = rest[NSA_SEL_PAST + 1:]
    b = pl.program_id(0)
    g = pl.program_id(1)
    s_idx = pl.program_id(2)
    S = pl.num_programs(2)
    R, L, HD = NSA_REP, NSA_BLOCK, NSA_HD
    bf16 = jnp.bfloat16
    n_past = NSA_SEL_PAST * L
    wb = win_ref.shape[0] // N_SLABS
    base = ((b * S + s_idx) * NSA_KV + g) * NSA_SEL_PAST
    scale = HD ** -0.5
    slopes = _row_slopes(g, R)
    qpos = PAST_LEN + s_idx

    @pl.when(s_idx == 0)
    def _():
        kw_s[...] = jnp.zeros_like(kw_s)
        vw_s[...] = jnp.zeros_like(vw_s)
        kw_s[0:wb, :] = win_ref[pl.ds(g * 2, wb, stride=N_SLABS), :].astype(bf16)
        vw_s[0:wb, :] = win_ref[pl.ds(g * 2 + 1, wb, stride=N_SLABS), :].astype(bf16)
        kw_s[wb:wb + NEW_ROWS, :] = new_ref[0, 0, 1, 0].astype(bf16)
        vw_s[wb:wb + NEW_ROWS, :] = new_ref[0, 0, 1, 1].astype(bf16)
        ks_s[n_past:, :] = jnp.zeros((SEL_SLOTS - n_past, HD), bf16)
        vs_s[n_past:, :] = jnp.zeros((SEL_SLOTS - n_past, HD), bf16)
        ks_s[n_past:n_past + NEW_ROWS, :] = new_ref[0, 0, 0, 0].astype(bf16)
        vs_s[n_past:n_past + NEW_ROWS, :] = new_ref[0, 0, 0, 1].astype(bf16)

    q = q_ref[0, 0].astype(bf16)
    gate = 1.0 / (1.0 + jnp.exp(-gate_ref[0, 0]))

    lane = lax.broadcasted_iota(jnp.int32, (1, SEL_SLOTS), 1)
    slot = lane // L
    kpos = jnp.where(lane >= n_past, PAST_LEN + (lane - n_past), lane % L)
    for k in range(NSA_SEL_PAST):
        ks_s[k * L:(k + 1) * L, :] = blk_refs[k][pl.ds(g * 2, L, stride=N_SLABS), :].astype(bf16)
        vs_s[k * L:(k + 1) * L, :] = blk_refs[k][pl.ds(g * 2 + 1, L, stride=N_SLABS), :].astype(bf16)
        kpos = kpos + jnp.where(slot == k, sel_ref[base + k] * L, 0)
    rel = qpos - kpos
    ok = (rel >= 0) & (lane < n_past + S)
    sc = _dot_nt(q, ks_s[...]) * scale - slopes * rel.astype(jnp.float32)
    sc = jnp.where(ok, sc, NEG_INF)
    p = jnp.exp(sc - jnp.max(sc, axis=-1, keepdims=True))
    p = p / jnp.sum(p, axis=-1, keepdims=True)
    p = jnp.where(ok, p, 0.0)
    o_s = jnp.dot(p.astype(bf16), vs_s[...], preferred_element_type=jnp.float32)

    lane_w = lax.broadcasted_iota(jnp.int32, (1, WIN_SLOTS), 1)
    kpos_w = jnp.where(lane_w >= wb, PAST_LEN + (lane_w - wb), PAST_LEN - wb + lane_w)
    rel_w = qpos - kpos_w
    ok_w = (rel_w >= 0) & (rel_w < NSA_WINDOW) & (lane_w < wb + S)
    sw = _dot_nt(q, kw_s[...]) * scale - slopes * rel_w.astype(jnp.float32)
    sw = jnp.where(ok_w, sw, NEG_INF)
    pw = jnp.exp(sw - jnp.max(sw, axis=-1, keepdims=True))
    pw = pw / jnp.sum(pw, axis=-1, keepdims=True)
    pw = jnp.where(ok_w, pw, 0.0)
    o_w = jnp.dot(pw.astype(bf16), vw_s[...], preferred_element_type=jnp.float32)

    o_ref[0, 0] = gate[:, 0:1] * oc_ref[0, 0] + gate[:, 1:2] * o_s + gate[:, 2:3] * o_w


def _nsa_sample_sel(phys_blk, sel_blk, q_rows, o_c, gate_rows, new_kv, win_rows, cache_blocks, layer):
    Bd, G, n_rows, HD = q_rows.shape
    S = n_rows // NSA_REP
    R = NSA_REP
    bf16 = jnp.bfloat16
    row_blk = lambda b, g, s, ph, se: (b, g, s, 0)

    def blk_spec(k):
        return pl.BlockSpec((None, None, SLAB_ROWS, HD),
                            lambda b, g, s, ph, se: (layer, ph[((b * S + s) * NSA_KV + g) * NSA_SEL_PAST + k], 0, 0))

    return pl.pallas_call(
        _nsa_sample_sel_kernel,
        grid_spec=pltpu.PrefetchScalarGridSpec(
            num_scalar_prefetch=2,
            grid=(Bd, G, S),
            in_specs=[pl.BlockSpec((1, 1, R, HD), row_blk),
                      pl.BlockSpec((1, 1, R, HD), row_blk),
                      pl.BlockSpec((1, 1, R, 3), row_blk),
                      pl.BlockSpec((1, 1, 2, 2, NEW_ROWS, HD), lambda b, g, s, ph, se: (b, g, 0, 0, 0, 0)),
                      pl.BlockSpec((None, None, win_rows.shape[2], HD), lambda b, g, s, ph, se: (layer, b, 0, 0))]
                     + [blk_spec(k) for k in range(NSA_SEL_PAST)],
            out_specs=pl.BlockSpec((1, 1, R, HD), row_blk),
            scratch_shapes=[pltpu.VMEM((SEL_SLOTS, HD), bf16), pltpu.VMEM((SEL_SLOTS, HD), bf16),
                            pltpu.VMEM((WIN_SLOTS, HD), bf16), pltpu.VMEM((WIN_SLOTS, HD), bf16)],
        ),
        out_shape=jax.ShapeDtypeStruct((Bd, G, n_rows, HD), jnp.float32),
        compiler_params=pltpu.CompilerParams(dimension_semantics=("arbitrary", "arbitrary", "arbitrary"),
                                             vmem_limit_bytes=VMEM_LIMIT_BYTES),
        name="nsa_sample_sel",
    )(phys_blk, sel_blk, q_rows, o_c, gate_rows, new_kv, win_rows, *([cache_blocks] * NSA_SEL_PAST))


GLA_C = 128
GLA_SUB = 16


def _gla_prompt_kernel(q_ref, k_ref, v_ref, g_ref, o_ref, st_ref, st_s, q_s, k_s, b_s, a_s):
    c = pl.program_id(2)
    C, SUB = GLA_C, GLA_SUB
    bf16 = jnp.bfloat16
    f32 = jnp.float32

    @pl.when(c == 0)
    def _():
        st_s[...] = jnp.zeros_like(st_s)

    q = q_ref[0] * (GLA_DK ** -0.5)
    k = k_ref[0]
    g = g_ref[0]
    vb = v_ref[0].astype(bf16)
    tri = jnp.where(lax.broadcasted_iota(jnp.int32, (C, C), 0) >= lax.broadcasted_iota(jnp.int32, (C, C), 1),
                    1.0, 0.0).astype(bf16)
    g_hi = g.astype(bf16)
    r1 = g - g_hi.astype(f32)
    g_mid = r1.astype(bf16)
    g_lo = (r1 - g_mid.astype(f32)).astype(bf16)
    b = (jnp.dot(tri, g_hi, preferred_element_type=f32) + jnp.dot(tri, g_mid, preferred_element_type=f32)
         + jnp.dot(tri, g_lo, preferred_element_type=f32))
    q_s[...] = q
    k_s[...] = k
    b_s[...] = b
    o = _dot_nt((q * jnp.exp(b)).astype(bf16), st_s[...].astype(bf16))

    lane = lax.broadcasted_iota(jnp.int32, (SUB, C), 1)
    trow = lax.broadcasted_iota(jnp.int32, (SUB, 1), 0)

    def sub_block(blk, carry):
        r0 = pl.multiple_of(blk * SUB, SUB)
        qi = q_s[pl.ds(r0, SUB), :]
        bi = b_s[pl.ds(r0, SUB), :]
        ref = jnp.where(blk > 0, b_s[pl.ds(jnp.maximum(r0 - 1, 0), 1), :], 0.0)
        qt = qi * jnp.exp(bi - ref)
        kt = k_s[...] * jnp.exp(jnp.minimum(ref - b_s[...], 0.0))
        a = jnp.where(lane < r0, _dot_nt(qt.astype(bf16), kt.astype(bf16)), 0.0)
        for s in range(SUB):
            ks = k_s[pl.ds(r0 + s, 1), :]
            bs = b_s[pl.ds(r0 + s, 1), :]
            col = jnp.sum(qi * jnp.exp(jnp.minimum(bi - bs, 0.0)) * ks, axis=-1, keepdims=True)
            a = jnp.where(lane == r0 + s, jnp.where(trow >= s, col, 0.0), a)
        a_s[pl.ds(r0, SUB), :] = a
        return carry

    lax.fori_loop(0, C // SUB, sub_block, 0)
    o_ref[0] = o + jnp.dot(a_s[...].astype(bf16), vb, preferred_element_type=f32)

    b_last = b[C - 1:C, :]
    kh = (k * jnp.exp(b_last - b)).astype(bf16)
    upd = lax.dot_general(vb, kh, (((0,), (0,)), ((), ())), preferred_element_type=f32)
    st_s[...] = st_s[...] * jnp.exp(b_last) + upd

    @pl.when(c == pl.num_programs(2) - 1)
    def _():
        st_ref[0, 0] = jnp.transpose(st_s[...])


def _gla_prompt(p, g):
    B, T, _ = p.shape
    H, DK, DV, C = GLA_HEADS, GLA_DK, GLA_DV, GLA_C
    f32 = jnp.float32
    return pl.pallas_call(
        _gla_prompt_kernel,
        grid=(B, H, T // C),
        in_specs=[pl.BlockSpec((1, C, DK), lambda b, h, c: (b, c, h)),
                  pl.BlockSpec((1, C, DK), lambda b, h, c: (b, c, H + h)),
                  pl.BlockSpec((1, C, DV), lambda b, h, c: (b, c, 2 * H * DK // DV + h)),
                  pl.BlockSpec((1, C, DK), lambda b, h, c: (b, c, h))],
        out_specs=[pl.BlockSpec((1, C, DV), lambda b, h, c: (b, c, h)),
                   pl.BlockSpec((1, 1, DK, DV), lambda b, h, c: (b, h, 0, 0))],
        out_shape=[jax.ShapeDtypeStruct((B, T, H * DV), f32), jax.ShapeDtypeStruct((B, H, DK, DV), f32)],
        scratch_shapes=[pltpu.VMEM((DV, DK), f32), pltpu.VMEM((C, DK), f32), pltpu.VMEM((C, DK), f32),
                        pltpu.VMEM((C, DK), f32), pltpu.VMEM((C, C), f32)],
        compiler_params=pltpu.CompilerParams(dimension_semantics=("arbitrary", "arbitrary", "arbitrary"),
                                             vmem_limit_bytes=VMEM_LIMIT_BYTES),
        name="gla_prompt",
    )(p, p, p, g)


def _rmsnorm(x, g):
    xf = x.astype(jnp.float32)
    y = xf * lax.rsqrt(jnp.mean(xf * xf, axis=-1, keepdims=True) + EPS)
    return y.astype(x.dtype) * g


def _alibi_slopes(n_heads, n_groups):
    h = jnp.arange(1, n_heads + 1, dtype=jnp.float32)
    return jnp.exp2(-8.0 * h / n_heads).reshape(n_groups, n_heads // n_groups)


def _adaln(c, w, b):
    m = jax.nn.silu(c) @ w + b
    return m.reshape(c.shape[0], 6, 1, D_MODEL)


def _attend(q, k, v, mask, dist, slopes, sink=None):
    s = jnp.einsum('...qgrd,...kgd->...qgrk', q, k).astype(jnp.float32) * (q.shape[-1] ** -0.5)
    s = s - slopes[:, :, None] * dist[..., :, None, None, :]
    m = mask[..., :, None, None, :]
    s = jnp.where(m, s, NEG_INF)
    if sink is not None:
        sk = jnp.broadcast_to(sink.astype(jnp.float32)[:, :, None], s.shape[:-1] + (1,))
        p = jax.nn.softmax(jnp.concatenate([s, sk], axis=-1), axis=-1)[..., :-1]
    else:
        p = jax.nn.softmax(s, axis=-1)
    p = jnp.where(m, p, 0.0)
    o = jnp.einsum('...qgrk,...kgd->...qgrd', p.astype(v.dtype), v)
    return o, p


def _window_prompt(q, kv, window, slopes, sink=None):
    B, T = q.shape[:2]
    nqb = T // QBLOCK
    n_prev = -(-(window - 1) // QBLOCK)
    kb = (n_prev + 1) * QBLOCK
    pad = jnp.pad(kv, [(0, 0), (n_prev * QBLOCK, 0)] + [(0, 0)] * (kv.ndim - 2))
    blocks = pad.reshape((B, nqb + n_prev, QBLOCK) + kv.shape[2:])
    band = jnp.concatenate([blocks[:, j:j + nqb] for j in range(n_prev + 1)], axis=2)
    qpos = jnp.arange(T).reshape(nqb, QBLOCK)
    kpos = (jnp.arange(nqb)[:, None] - n_prev) * QBLOCK + jnp.arange(kb)[None, :]
    rel = qpos[:, :, None] - kpos[:, None, :]
    mask = (kpos[:, None, :] >= 0) & (rel >= 0) & (rel < window)
    qb = q.reshape((B, nqb, QBLOCK) + q.shape[2:])
    o, _ = _attend(qb, band[..., 0, :], band[..., 1, :], mask, rel.astype(jnp.float32), slopes, sink)
    return o.reshape(q.shape)


def _window_sample(q, kv_new, buf, window, slopes, sink=None):
    S = q.shape[1]
    wb = buf.shape[1]
    kv = jnp.concatenate([buf.astype(kv_new.dtype), kv_new], axis=1)
    qpos = PAST_LEN + jnp.arange(S)
    kpos = PAST_LEN - wb + jnp.arange(wb + S)
    rel = qpos[:, None] - kpos[None, :]
    mask = (rel >= 0) & (rel < window)
    o, _ = _attend(q, kv[..., 0, :], kv[..., 1, :], mask, rel.astype(jnp.float32), slopes, sink)
    return o, kv[:, S:]


def _sel_attend(q, k, v, kpos, qpos, slopes):
    s = jnp.einsum('bqgrd,bqgkd->bqgrk', q, k).astype(jnp.float32) * (q.shape[-1] ** -0.5)
    rel = qpos[None, :, None, None] - kpos
    s = s - slopes[None, None, :, :, None] * rel[:, :, :, None, :].astype(jnp.float32)
    m = (rel >= 0)[:, :, :, None, :]
    p = jnp.where(m, jax.nn.softmax(jnp.where(m, s, NEG_INF), axis=-1), 0.0)
    return jnp.einsum('bqgrk,bqgkd->bqgrd', p.astype(v.dtype), v)


def _nsa_project(h, w_in):
    B, T = h.shape[:2]
    p = h @ w_in
    nq = NSA_HEADS * NSA_HD
    nkv = 2 * NSA_KV * NSA_HD
    q = p[..., :nq].reshape(B, T, NSA_KV, NSA_REP, NSA_HD)
    kv = p[..., nq:nq + 3 * nkv].reshape(B, T, 3, NSA_KV, 2, NSA_HD)
    gates = jax.nn.sigmoid(p[..., nq + 3 * nkv:].reshape(B, T, 3, NSA_KV, NSA_REP))
    return q, kv[:, :, 0], kv[:, :, 1], kv[:, :, 2], gates


def _nsa_compress(blocks, pe, w1, w2):
    x = blocks + pe[:, None]
    hid = jax.nn.silu(jnp.einsum('...nlgcd,cldh->...ngch', x, w1))
    return jnp.einsum('...ngch,chd->...ngcd', hid, w2)


def _nsa_cmp_attend(q, cmp, qpos, slopes):
    nb = cmp.shape[-4]
    j = jnp.arange(nb)
    mask = j[None, :] < (qpos // NSA_BLOCK)[:, None]
    dist = (qpos[:, None] - (j[None, :] * NSA_BLOCK + NSA_BLOCK - 1)).astype(jnp.float32)
    o, p = _attend(q, cmp[..., 0, :], cmp[..., 1, :], mask, dist, slopes)
    return o, p.sum(axis=-2)


def _nsa_merge(gates, o_c, o_s, o_w, w_out):
    o = gates[:, :, 0, :, :, None] * o_c + gates[:, :, 1, :, :, None] * o_s + gates[:, :, 2, :, :, None] * o_w
    return o.reshape(o.shape[0], o.shape[1], -1) @ w_out


def _nsa_layer(hp, hs, cache_cmp, cache_slc, win_all, page_table, li, w_in, pe, w1, w2, w_out):
    L = NSA_BLOCK
    B, T = hp.shape[:2]
    nq = NSA_HEADS * NSA_HD
    nkv = 2 * NSA_KV * NSA_HD
    kv_shape = (NSA_KV, 2, NSA_HD)
    pe2, w1bd, w2bd = _compress_weights(pe, w1, w2)
    pp = hp @ w_in
    kvc_rows = pp[..., nq:nq + nkv]
    kvc = kvc_rows.reshape((B, T) + kv_shape)
    kvs = pp[..., nq + nkv:nq + 2 * nkv].reshape((B, T) + kv_shape)
    win_p = pp[:, T - min(NSA_WINDOW, T):, nq + 2 * nkv:nq + 3 * nkv].reshape((B, min(NSA_WINDOW, T)) + kv_shape)
    cmp_p = _nsa_compress_dense(kvc_rows.reshape(B * T * N_SLABS, NSA_HD), pe2, w1bd, w2bd).reshape(B, T // L, nkv)
    gl = pp[..., nq + 3 * nkv:].reshape(B, T, 3, NSA_KV, NSA_REP).transpose(0, 3, 1, 2, 4)
    o_p = _nsa_prompt_attention(pp, cmp_p, gl.reshape(B, NSA_KV, T, 3 * NSA_REP))
    yp = o_p @ w_out

    Bd, S = hs.shape[:2]
    G, R, HD = NSA_KV, NSA_REP, NSA_HD
    ps = hs @ w_in
    kvc_s = ps[..., nq:nq + nkv].reshape((Bd, S) + kv_shape)
    kvs_s = ps[..., nq + nkv:nq + 2 * nkv].reshape((Bd, S) + kv_shape)
    kvw_s = ps[..., nq + 2 * nkv:nq + 3 * nkv].reshape((Bd, S) + kv_shape)
    q_rows = ps[..., :nq].reshape(Bd, S, G, R, HD).transpose(0, 2, 1, 3, 4).reshape(Bd, G, S * R, HD)
    gate_rows = ps[..., nq + 3 * nkv:].reshape(Bd, S, 3, G, R).transpose(0, 3, 1, 4, 2).reshape(Bd, G, S * R, 3)
    new_kv = ps[..., nq + nkv:nq + 3 * nkv].reshape(Bd, S, 2, G, 2, HD).transpose(0, 3, 2, 4, 1, 5)
    new_kv = jnp.pad(new_kv, [(0, 0)] * 4 + [(0, NEW_ROWS - S), (0, 0)])
    cache_rows = cache_cmp.reshape(cache_cmp.shape[:2] + (PAGE_SIZE * N_SLABS, HD))
    cmp = _nsa_compress_paged(cache_rows, li, page_table.reshape(-1), pe2, w1bd, w2bd)
    o_c, sel_out = _nsa_sample_cmp(q_rows, cmp)
    sel = sel_out[:, :, :S, :NSA_SEL_PAST].transpose(0, 2, 1, 3)
    bpp = PAGE_SIZE // L
    phys = page_table[jnp.arange(Bd)[:, None, None, None], sel // bpp] * bpp + sel % bpp
    cache_blocks = cache_slc.reshape(cache_slc.shape[0], cache_slc.shape[1] * bpp, SLAB_ROWS, HD)
    win_rows = win_all.reshape(win_all.shape[:2] + (win_all.shape[2] * N_SLABS, HD))
    o_s = _nsa_sample_sel(phys.reshape(-1), sel.reshape(-1), q_rows, o_c, gate_rows, new_kv, win_rows,
                          cache_blocks, li)
    ys = o_s.reshape(Bd, G, S, R, HD).transpose(0, 2, 1, 3, 4).reshape(Bd, S, nq) @ w_out
    win_s = jnp.concatenate([win_all[li], kvw_s], axis=1)[:, S:]
    return yp, ys, (kvc, kvc_s, kvs, kvs_s, win_p, win_s)


def _swa_project(h, w_in, b_in):
    B, T = h.shape[:2]
    p = h @ w_in + b_in
    nq = SWA_HEADS * SWA_HD
    q = p[..., :nq].reshape(B, T, SWA_KV, SWA_REP, SWA_HD)
    kv = p[..., nq:].reshape(B, T, SWA_KV, 2, SWA_HD)
    return q, kv


def _swa_layer(hp, hs, buf, w_in, b_in, sinks, w_out, b_out, slopes):
    sink = sinks.reshape(SWA_KV, SWA_REP)
    B, T = hp.shape[:2]
    q, kv = _swa_project(hp, w_in, b_in)
    o = _window_prompt(q, kv, SWA_WINDOW, slopes, sink)
    yp = o.reshape(B, T, -1) @ w_out + b_out
    buf_p = kv[:, T - min(SWA_WINDOW, T):]
    Bd, S = hs.shape[:2]
    q, kv_s = _swa_project(hs, w_in, b_in)
    o, buf_s = _window_sample(q, kv_s, buf, SWA_WINDOW, slopes, sink)
    ys = o.reshape(Bd, S, -1) @ w_out + b_out
    return yp, ys, (buf_p, buf_s)


def _gla_project(h, w_in, w_a2, b_a):
    B, T = h.shape[:2]
    p = h @ w_in
    nk = GLA_HEADS * GLA_DK
    nv = GLA_HEADS * GLA_DV
    q = p[..., :nk].reshape(B, T, GLA_HEADS, GLA_DK).astype(jnp.float32) * (GLA_DK ** -0.5)
    k = p[..., nk:2 * nk].reshape(B, T, GLA_HEADS, GLA_DK).astype(jnp.float32)
    v = p[..., 2 * nk:2 * nk + nv].reshape(B, T, GLA_HEADS, GLA_DV).astype(jnp.float32)
    r = p[..., 2 * nk + nv:2 * nk + 2 * nv].reshape(B, T, GLA_HEADS, GLA_DV)
    a = (p[..., 2 * nk + 2 * nv:] @ w_a2 + b_a).astype(jnp.float32)
    g = (jax.nn.log_sigmoid(a) / GLA_NORMALIZER).reshape(B, T, GLA_HEADS, GLA_DK)
    return q, k, v, g, r


def _gla_chunk(state, q, k, v, g):
    C = q.shape[1]
    b = jnp.cumsum(g, axis=1)
    causal = jnp.tril(jnp.ones((C, C), dtype=bool))
    o_inter = jnp.einsum('bthk,bhkv->bthv', q * jnp.exp(b), state)
    diff = b[:, :, None] - b[:, None, :]
    decay = jnp.exp(jnp.where(causal[None, :, :, None, None], diff, -jnp.inf))
    a = jnp.einsum('bthk,btshk,bshk->btsh', q, decay, k)
    o_intra = jnp.einsum('btsh,bshv->bthv', a, v)
    b_last = b[:, -1]
    new_state = jnp.exp(b_last)[..., None] * state + jnp.einsum('bshk,bshv->bhkv', k * jnp.exp(b_last[:, None] - b), v)
    return new_state, o_inter + o_intra


def _gla_layer(hp, hs, state, w_in, w_a2, b_a, norm, w_out):
    def readout(o, r, h):
        y = _rmsnorm(o, norm).astype(h.dtype) * jax.nn.silu(r)
        return y.reshape(h.shape[0], h.shape[1], -1) @ w_out

    B, T = hp.shape[:2]
    nk = GLA_HEADS * GLA_DK
    nv = GLA_HEADS * GLA_DV
    pp = hp @ w_in
    a = (pp[..., 2 * nk + 2 * nv:] @ w_a2 + b_a).astype(jnp.float32)
    g = jax.nn.log_sigmoid(a) / GLA_NORMALIZER
    o, s_p = _gla_prompt(pp, g)
    r = pp[..., 2 * nk + nv:2 * nk + 2 * nv].reshape(B, T, GLA_HEADS, GLA_DV)
    yp = readout(o.reshape(B, T, GLA_HEADS, GLA_DV), r, hp)
    q, k, v, g, r = _gla_project(hs, w_in, w_a2, b_a)
    s_s, o = _gla_chunk(state.astype(jnp.float32), q, k, v, g)
    ys = readout(o, r, hs)
    return yp, ys, (s_p.astype(hp.dtype), s_s.astype(state.dtype))


def kernel(x_prompt, x_sample, cache_nsa_cmp, cache_nsa_slc, state_nsa_win, state_swa_kv, state_gla,
           page_table, c_prompt, c_sample, ada_w, ada_b, norm_mix, norm_ffn, norm_final,
           nsa_w_in, nsa_cmp_pe, nsa_cmp_w1, nsa_cmp_w2, nsa_w_out,
           swa_w_in, swa_b_in, swa_sinks, swa_w_out, swa_b_out,
           gla_w_in, gla_w_a2, gla_b_a, gla_norm, gla_w_out,
           moe_w_router, moe_b_router, moe_w1, moe_b1, moe_w2, moe_b2):
    B, T = x_prompt.shape[:2]
    Bd, S = x_sample.shape[:2]
    n_p = B * T
    nsa_slopes = _alibi_slopes(NSA_HEADS, NSA_KV)
    swa_slopes = _alibi_slopes(SWA_HEADS, SWA_KV)
    xp, xs = x_prompt, x_sample
    nsa_new, swa_new, gla_new = [], [], []
    for i in range(DEPTH):
        mp = _adaln(c_prompt, ada_w[i], ada_b[i])
        ms = _adaln(c_sample, ada_w[i], ada_b[i])
        hp = _rmsnorm(xp, norm_mix[i]) * (1.0 + mp[:, 1]) + mp[:, 0]
        hs = _rmsnorm(xs, norm_mix[i]) * (1.0 + ms[:, 1]) + ms[:, 0]
        kind, li = i % N_MIXERS, i // N_MIXERS
        if kind == 0:
            yp, ys, st = _nsa_layer(hp, hs, cache_nsa_cmp, cache_nsa_slc, state_nsa_win, page_table, li,
                                    nsa_w_in[li], nsa_cmp_pe[li], nsa_cmp_w1[li], nsa_cmp_w2[li], nsa_w_out[li])
            nsa_new.append(st)
        elif kind == 1:
            yp, ys, st = _swa_layer(hp, hs, state_swa_kv[li], swa_w_in[li], swa_b_in[li], swa_sinks[li],
                                    swa_w_out[li], swa_b_out[li], swa_slopes)
            swa_new.append(st)
        else:
            yp, ys, st = _gla_layer(hp, hs, state_gla[li], gla_w_in[li], gla_w_a2[li], gla_b_a[li],
                                    gla_norm[li], gla_w_out[li])
            gla_new.append(st)
        xp = xp + mp[:, 2] * yp
        xs = xs + ms[:, 2] * ys
        hp = _rmsnorm(xp, norm_ffn[i]) * (1.0 + mp[:, 4]) + mp[:, 3]
        hs = _rmsnorm(xs, norm_ffn[i]) * (1.0 + ms[:, 4]) + ms[:, 3]
        h_all = jnp.concatenate([hp.reshape(n_p, D_MODEL), hs.reshape(Bd * S, D_MODEL)], axis=0)
        y_all = _moe(h_all, moe_w_router[i], moe_b_router[i], i, moe_w1, moe_b1, moe_w2, moe_b2)
        xp = xp + mp[:, 5] * y_all[:n_p].reshape(B, T, D_MODEL)
        xs = xs + ms[:, 5] * y_all[n_p:].reshape(Bd, S, D_MODEL)
    y_prompt = _rmsnorm(xp, norm_final)
    y_sample = _rmsnorm(xs, norm_final)
    stack = lambda sts, k: jnp.stack([st[k] for st in sts])
    return (y_prompt, y_sample, stack(nsa_new, 0), stack(nsa_new, 1), stack(nsa_new, 2), stack(nsa_new, 3),
            stack(nsa_new, 4), stack(nsa_new, 5), stack(swa_new, 0), stack(swa_new, 1),
            stack(gla_new, 0), stack(gla_new, 1))
```

```python
import functools

import jax
import jax.numpy as jnp
from jax import lax
from jax.experimental import pallas as pl
from jax.experimental.pallas import tpu as pltpu

D_MODEL = 2048
DEPTH = 4
PAST_LEN = 16384
PAGE_SIZE = 128
N_MIXERS = 3

NSA_HEADS = 16
NSA_KV = 2
NSA_HD = D_MODEL // NSA_HEADS
NSA_REP = NSA_HEADS // NSA_KV
NSA_BLOCK = 64
NSA_TOPN = 16
NSA_WINDOW = 512
NSA_SEL_QB = 64
SWA_HEADS = 32
SWA_KV = 4
SWA_HD = D_MODEL // SWA_HEADS
SWA_REP = SWA_HEADS // SWA_KV
SWA_WINDOW = 128
GLA_HEADS = 4
GLA_DK = D_MODEL // 2 // GLA_HEADS
GLA_DV = D_MODEL // GLA_HEADS
GLA_NORMALIZER = 16.0
GLA_CHUNK = 64
N_EXPERTS = 32
TOP_K = 4
D_FF = D_MODEL
SWIGLU_LIMIT = 7.0
SWIGLU_ALPHA = 1.702
QBLOCK = 128
NEG_INF = -1e30
EPS = 1e-6

VMEM_LIMIT_BYTES = 56 * 1024 * 1024
MOE_TM = 256
MOE_TF = 1024
MOE_TN = 1024
CAST_ROWS = 256


def _cast_rows(src_ref, dst_ref):
    n = dst_ref.shape[0] // CAST_ROWS

    def body(c, carry):
        r = pl.multiple_of(c * CAST_ROWS, CAST_ROWS)
        dst_ref[pl.ds(r, CAST_ROWS), :] = src_ref[pl.ds(r, CAST_ROWS), :].astype(jnp.bfloat16)
        return carry

    lax.fori_loop(0, n, body, 0)


def _stream_slab(j, i, n_j, meta, slab_copies, convert):
    te_ref, tf_ref, _, nx_ref, lg_ref, gi_ref, ng_ref = meta

    @pl.when(tf_ref[i] == 1)
    def _():
        slot = lax.rem(j * ng_ref[0] + gi_ref[i], 2)

        @pl.when((j == 0) & (i == 0))
        def _():
            for cp in slab_copies(te_ref[0], 0, slot):
                cp.start()

        for cp in slab_copies(te_ref[i], j, slot):
            cp.wait()
        convert(slot)
        nxt_j = j + lg_ref[i]

        @pl.when(nxt_j < n_j)
        def _():
            for cp in slab_copies(nx_ref[i], nxt_j, 1 - slot):
                cp.start()


def _moe_up_kernel(*refs, layer):
    meta, (x_ref, w1_ref, bg_ref, bu_ref, h_ref, wbuf, wg_s, wu_s, sem) = refs[:7], refs[7:]
    tv_ref = meta[2]
    j = pl.program_id(0)
    i = pl.program_id(1)

    def slab_copies(e, jj, slot):
        col = pl.multiple_of(jj * MOE_TF, MOE_TF)
        return [pltpu.make_async_copy(w1_ref.at[layer, e, :, pl.ds(c0 + col, MOE_TF)], wbuf.at[slot, c], sem.at[slot])
                for c, c0 in enumerate((0, D_FF))]

    def convert(slot):
        _cast_rows(wbuf.at[slot, 0], wg_s)
        _cast_rows(wbuf.at[slot, 1], wu_s)

    _stream_slab(j, i, pl.num_programs(0), meta, slab_copies, convert)

    @pl.when(tv_ref[i] == 1)
    def _():
        x = x_ref[...].astype(jnp.bfloat16)
        g = jnp.dot(x, wg_s[...], preferred_element_type=jnp.float32) + bg_ref[...]
        u = jnp.dot(x, wu_s[...], preferred_element_type=jnp.float32) + bu_ref[...]
        g = jnp.minimum(g, SWIGLU_LIMIT)
        u = jnp.clip(u, -SWIGLU_LIMIT, SWIGLU_LIMIT)
        sig = 1.0 / (1.0 + jnp.exp(-SWIGLU_ALPHA * g))
        h_ref[...] = (g * sig * (u + 1.0)).astype(h_ref.dtype)

    @pl.when(tv_ref[i] == 0)
    def _():
        h_ref[...] = jnp.zeros_like(h_ref)


def _moe_down_kernel(*refs, layer):
    meta, (h_ref, w2_ref, b_ref, y_ref, wbuf, w_s, sem) = refs[:7], refs[7:]
    tv_ref = meta[2]
    j = pl.program_id(0)
    i = pl.program_id(1)

    def slab_copies(e, jj, slot):
        col = pl.multiple_of(jj * MOE_TN, MOE_TN)
        return [pltpu.make_async_copy(w2_ref.at[layer, e, :, pl.ds(col, MOE_TN)], wbuf.at[slot], sem.at[slot])]

    _stream_slab(j, i, pl.num_programs(0), meta, slab_copies, lambda slot: _cast_rows(wbuf.at[slot], w_s))

    @pl.when(tv_ref[i] == 1)
    def _():
        y_ref[...] = jnp.dot(h_ref[...], w_s[...], preferred_element_type=jnp.float32) + b_ref[...]

    @pl.when(tv_ref[i] == 0)
    def _():
        y_ref[...] = jnp.zeros_like(y_ref)


def _moe_ffn(x_sorted, meta, layer, w1, b1, w2, b2):
    P, D = x_sorted.shape
    n_tiles = P // MOE_TM
    nj1 = D_FF // MOE_TF
    b1r = b1.reshape(DEPTH, N_EXPERTS, 1, 2 * D_FF)
    b2r = b2.reshape(DEPTH, N_EXPERTS, 1, D)
    params = pltpu.CompilerParams(dimension_semantics=("arbitrary", "arbitrary"),
                                  vmem_limit_bytes=VMEM_LIMIT_BYTES)
    bf16 = jnp.bfloat16
    h = pl.pallas_call(
        functools.partial(_moe_up_kernel, layer=layer),
        grid_spec=pltpu.PrefetchScalarGridSpec(
            num_scalar_prefetch=7,
            grid=(nj1, n_tiles),
            in_specs=[
                pl.BlockSpec((MOE_TM, D), lambda j, i, *m: (i, 0)),
                pl.BlockSpec(memory_space=pl.ANY),
                pl.BlockSpec((None, None, 1, MOE_TF), lambda j, i, te, *m: (layer, te[i], 0, j)),
                pl.BlockSpec((None, None, 1, MOE_TF), lambda j, i, te, *m: (layer, te[i], 0, nj1 + j)),
            ],
            out_specs=pl.BlockSpec((MOE_TM, MOE_TF), lambda j, i, *m: (i, j)),
            scratch_shapes=[pltpu.VMEM((2, 2, D, MOE_TF), jnp.float32),
                            pltpu.VMEM((D, MOE_TF), bf16), pltpu.VMEM((D, MOE_TF), bf16),
                            pltpu.SemaphoreType.DMA((2,))],
        ),
        out_shape=jax.ShapeDtypeStruct((P, D_FF), bf16),
        compiler_params=params,
        name="moe_up",
    )(*meta, x_sorted, w1, b1r, b1r)
    y = pl.pallas_call(
        functools.partial(_moe_down_kernel, layer=layer),
        grid_spec=pltpu.PrefetchScalarGridSpec(
            num_scalar_prefetch=7,
            grid=(D // MOE_TN, n_tiles),
            in_specs=[
                pl.BlockSpec((MOE_TM, D_FF), lambda j, i, *m: (i, 0)),
                pl.BlockSpec(memory_space=pl.ANY),
                pl.BlockSpec((None, None, 1, MOE_TN), lambda j, i, te, *m: (layer, te[i], 0, j)),
            ],
            out_specs=pl.BlockSpec((MOE_TM, MOE_TN), lambda j, i, *m: (i, j)),
            scratch_shapes=[pltpu.VMEM((2, D_FF, MOE_TN), jnp.float32), pltpu.VMEM((D_FF, MOE_TN), bf16),
                            pltpu.SemaphoreType.DMA((2,))],
        ),
        out_shape=jax.ShapeDtypeStruct((P, D), jnp.float32),
        compiler_params=params,
        name="moe_down",
    )(*meta, h, w2, b2r)
    return y


COMBINE_TOKENS = 64


def _row_copy(src, src_row, dst, dst_row, sem):
    return pltpu.make_async_copy(src.at[pl.ds(src_row, 1), :], dst.at[pl.ds(dst_row, 1), :], sem)


def _moe_dispatch_kernel(tok_ref, x_ref, out_ref, buf, sem):
    i = pl.program_id(0)
    n_steps = pl.num_programs(0)
    slot = lax.rem(i, 2)

    def for_rows(step, slot_, act):
        def body(r, carry):
            act(_row_copy(x_ref, tok_ref[step * MOE_TM + r], buf.at[slot_], r, sem.at[slot_]))
            return carry
        lax.fori_loop(0, MOE_TM, body, 0)

    @pl.when(i == 0)
    def _():
        for_rows(0, 0, lambda cp: cp.start())

    @pl.when(i + 1 < n_steps)
    def _():
        for_rows(i + 1, 1 - slot, lambda cp: cp.start())

    for_rows(i, slot, lambda cp: cp.wait())
    out_ref[...] = buf[slot].astype(out_ref.dtype)


def _moe_dispatch(x, row_tok):
    D = x.shape[1]
    P = row_tok.shape[0]
    return pl.pallas_call(
        _moe_dispatch_kernel,
        grid_spec=pltpu.PrefetchScalarGridSpec(
            num_scalar_prefetch=1,
            grid=(P // MOE_TM,),
            in_specs=[pl.BlockSpec(memory_space=pl.ANY)],
            out_specs=pl.BlockSpec((MOE_TM, D), lambda i, t: (i, 0)),
            scratch_shapes=[pltpu.VMEM((2, MOE_TM, D), x.dtype), pltpu.SemaphoreType.DMA((2,))],
        ),
        out_shape=jax.ShapeDtypeStruct((P, D), jnp.bfloat16),
        compiler_params=pltpu.CompilerParams(dimension_semantics=("arbitrary",),
                                             vmem_limit_bytes=VMEM_LIMIT_BYTES),
        name="moe_dispatch",
    )(row_tok, x)


def _moe_combine_kernel(dest_ref, y_ref, gate_ref, out_ref, buf, sem):
    i = pl.program_id(0)
    n_steps = pl.num_programs(0)
    slot = lax.rem(i, 2)
    CT = COMBINE_TOKENS

    def for_rows(step, slot_, act):
        def body(t, carry):
            for k in range(TOP_K):
                a = (step * CT + t) * TOP_K + k
                act(_row_copy(y_ref, dest_ref[a], buf.at[slot_], k * CT + t, sem.at[slot_]))
            return carry
        lax.fori_loop(0, CT, body, 0)

    @pl.when(i == 0)
    def _():
        for_rows(0, 0, lambda cp: cp.start())

    @pl.when(i + 1 < n_steps)
    def _():
        for_rows(i + 1, 1 - slot, lambda cp: cp.start())

    for_rows(i, slot, lambda cp: cp.wait())
    gate = gate_ref[...]
    acc = buf[slot, 0:CT, :] * gate[:, 0:1]
    for k in range(1, TOP_K):
        acc = acc + buf[slot, k * CT:(k + 1) * CT, :] * gate[:, k:k + 1]
    out_ref[...] = acc


def _moe_combine(y, dest, gate):
    D = y.shape[1]
    N = gate.shape[0]
    CT = COMBINE_TOKENS
    return pl.pallas_call(
        _moe_combine_kernel,
        grid_spec=pltpu.PrefetchScalarGridSpec(
            num_scalar_prefetch=1,
            grid=(N // CT,),
            in_specs=[pl.BlockSpec(memory_space=pl.ANY), pl.BlockSpec((CT, TOP_K), lambda i, d: (i, 0))],
            out_specs=pl.BlockSpec((CT, D), lambda i, d: (i, 0)),
            scratch_shapes=[pltpu.VMEM((2, CT * TOP_K, D), jnp.float32), pltpu.SemaphoreType.DMA((2,))],
        ),
        out_shape=jax.ShapeDtypeStruct((N, D), jnp.float32),
        compiler_params=pltpu.CompilerParams(dimension_semantics=("arbitrary",),
                                             vmem_limit_bytes=VMEM_LIMIT_BYTES),
        name="moe_combine",
    )(dest, y, gate)


def _moe(x, w_r, b_r, layer, w1, b1, w2, b2):
    N, D = x.shape
    logits = (x @ w_r + b_r).astype(jnp.float32)
    top_v, top_i = lax.top_k(logits, TOP_K)
    gate = jax.nn.softmax(top_v, axis=-1)
    A = N * TOP_K
    e_flat = top_i.reshape(-1).astype(jnp.int32)
    onehot = (e_flat[:, None] == jnp.arange(N_EXPERTS, dtype=jnp.int32)[None, :]).astype(jnp.int32)
    csum = jnp.cumsum(onehot, axis=0)
    counts = csum[-1]
    rank = jnp.take_along_axis(csum, e_flat[:, None], axis=1)[:, 0] - 1
    padded = (counts + MOE_TM - 1) // MOE_TM * MOE_TM
    pend = jnp.cumsum(padded)
    pstart = pend - padded
    dest = pstart[e_flat] + rank
    n_tiles = -(-(A + N_EXPERTS * (MOE_TM - 1)) // MOE_TM)
    P = n_tiles * MOE_TM
    tile_start = jnp.arange(n_tiles, dtype=jnp.int32) * MOE_TM
    tile_valid = tile_start < pend[-1]
    tile_e = jnp.minimum(jnp.searchsorted(pend, tile_start, side='right'), N_EXPERTS - 1).astype(jnp.int32)
    prev_e = jnp.concatenate([jnp.full((1,), -1, jnp.int32), tile_e[:-1]])
    tile_first = ((tile_e != prev_e) & tile_valid).astype(jnp.int32)
    group_idx = jnp.cumsum(tile_first) - 1
    n_groups = jnp.sum(tile_first)
    group_e = jnp.argsort(counts == 0, stable=True).astype(jnp.int32)
    next_e = group_e[(group_idx + 1) % n_groups]
    last_group = (group_idx + 1 == n_groups).astype(jnp.int32)
    meta = (tile_e, tile_first, tile_valid.astype(jnp.int32), next_e, last_group, group_idx.astype(jnp.int32),
            n_groups.reshape(1).astype(jnp.int32))
    tok_flat = jnp.repeat(jnp.arange(N, dtype=jnp.int32), TOP_K)
    row_tok = jnp.zeros((P,), jnp.int32).at[dest].set(tok_flat)
    x_sorted = _moe_dispatch(x, row_tok)
    y = _moe_ffn(x_sorted, meta, layer, w1, b1, w2, b2)
    return _moe_combine(y, dest, gate)


def _dot_nt(a, b):
    return lax.dot_general(a, b, (((1,), (1,)), ((), ())), preferred_element_type=jnp.float32)


N_SLABS = 2 * NSA_KV
SLAB_ROWS = NSA_BLOCK * N_SLABS
CMP_DENSE_BLOCKS = 64
CMP_PAGES = 64
CMP_UNROLL = 8


def _compress_rows(load_row, n_blocks, pe_ref, w1_ref, w2_ref, out_ref):
    W = 2 * NSA_HD
    for g in range(NSA_KV):
        def step(l, acc):
            x = jnp.concatenate([load_row(l, 2 * g), load_row(l, 2 * g + 1)], axis=1)
            x = (x + pe_ref[pl.ds(l, 1), :]).astype(jnp.bfloat16)
            return acc + jnp.dot(x, w1_ref[l], preferred_element_type=jnp.float32)

        hid = lax.fori_loop(0, NSA_BLOCK, step, jnp.zeros((n_blocks, W), jnp.float32), unroll=CMP_UNROLL)
        hid = hid * (1.0 / (1.0 + jnp.exp(-hid)))
        out_ref[:, g * W:(g + 1) * W] = jnp.dot(hid.astype(jnp.bfloat16), w2_ref[...],
                                                preferred_element_type=jnp.float32)


def _compress_weights(pe, w1, w2):
    z1 = jnp.zeros_like(w1[0])
    w1bd = jnp.concatenate([jnp.concatenate([w1[0], z1], axis=2), jnp.concatenate([z1, w1[1]], axis=2)], axis=1)
    z2 = jnp.zeros_like(w2[0])
    w2bd = jnp.concatenate([jnp.concatenate([w2[0], z2], axis=1), jnp.concatenate([z2, w2[1]], axis=1)], axis=0)
    return pe.reshape(NSA_BLOCK, 2 * NSA_HD), w1bd.astype(jnp.bfloat16), w2bd.astype(jnp.bfloat16)


def _compress_dense_kernel(x_ref, pe_ref, w1_ref, w2_ref, out_ref):
    n_blocks = out_ref.shape[0]
    load = lambda l, j: x_ref[pl.ds(l * N_SLABS + j, n_blocks, stride=SLAB_ROWS), :]
    _compress_rows(load, n_blocks, pe_ref, w1_ref, w2_ref, out_ref)


def _nsa_compress_dense(kv_slab_rows, pe2, w1bd, w2bd):
    nblk = kv_slab_rows.shape[0] // SLAB_ROWS
    step = min(nblk, CMP_DENSE_BLOCKS)
    W4 = N_SLABS * NSA_HD
    const = lambda shape: pl.BlockSpec(shape, lambda i: (0,) * len(shape))
    return pl.pallas_call(
        _compress_dense_kernel,
        grid=(nblk // step,),
        in_specs=[pl.BlockSpec((step * SLAB_ROWS, NSA_HD), lambda i: (i, 0)),
                  const(pe2.shape), const(w1bd.shape), const(w2bd.shape)],
        out_specs=pl.BlockSpec((step, W4), lambda i: (i, 0)),
        out_shape=jax.ShapeDtypeStruct((nblk, W4), jnp.float32),
        compiler_params=pltpu.CompilerParams(dimension_semantics=("arbitrary",),
                                             vmem_limit_bytes=VMEM_LIMIT_BYTES),
        name="nsa_compress_dense",
    )(kv_slab_rows, pe2, w1bd, w2bd)


def _page_copy(pid_ref, cache_ref, buf, sem, layer, step, slot, p):
    rows = cache_ref.shape[2]
    pid = pid_ref[step * CMP_PAGES + p]
    return pltpu.make_async_copy(cache_ref.at[layer, pid], buf.at[slot, pl.ds(p * rows, rows), :], sem.at[slot])


def _compress_paged_kernel(pid_ref, cache_ref, pe_ref, w1_ref, w2_ref, out_ref, buf, sem, *, layer):
    i = pl.program_id(0)
    n_steps = pl.num_programs(0)
    slot = lax.rem(i, 2)

    def for_pages(step, slot_, act):
        def body(p, carry):
            act(_page_copy(pid_ref, cache_ref, buf, sem, layer, step, slot_, p))
            return carry
        lax.fori_loop(0, CMP_PAGES, body, 0)

    @pl.when(i == 0)
    def _():
        for_pages(0, 0, lambda cp: cp.start())

    @pl.when(i + 1 < n_steps)
    def _():
        for_pages(i + 1, 1 - slot, lambda cp: cp.start())

    for_pages(i, slot, lambda cp: cp.wait())
    n_blocks = out_ref.shape[0]
    load = lambda l, j: buf[slot, pl.ds(l * N_SLABS + j, n_blocks, stride=SLAB_ROWS), :]
    _compress_rows(load, n_blocks, pe_ref, w1_ref, w2_ref, out_ref)


def _nsa_compress_paged(cache, layer, page_ids, pe2, w1bd, w2bd):
    page_rows = cache.shape[2]
    n_pages = page_ids.shape[0]
    bpp = page_rows // SLAB_ROWS
    W4 = N_SLABS * NSA_HD
    const = lambda shape: pl.BlockSpec(shape, lambda i, pid: (0,) * len(shape))
    return pl.pallas_call(
        functools.partial(_compress_paged_kernel, layer=layer),
        grid_spec=pltpu.PrefetchScalarGridSpec(
            num_scalar_prefetch=1,
            grid=(n_pages // CMP_PAGES,),
            in_specs=[pl.BlockSpec(memory_space=pl.ANY), const(pe2.shape), const(w1bd.shape), const(w2bd.shape)],
            out_specs=pl.BlockSpec((CMP_PAGES * bpp, W4), lambda i, pid: (i, 0)),
            scratch_shapes=[pltpu.VMEM((2, CMP_PAGES * page_rows, NSA_HD), jnp.float32),
                            pltpu.SemaphoreType.DMA((2,))],
        ),
        out_shape=jax.ShapeDtypeStruct((n_pages * bpp, W4), jnp.float32),
        compiler_params=pltpu.CompilerParams(dimension_semantics=("arbitrary",),
                                             vmem_limit_bytes=VMEM_LIMIT_BYTES),
        name="nsa_compress_paged",
    )(page_ids, cache, pe2, w1bd, w2bd)


NSA_TQ = 128
NSA_TK = 512
LOG2E = 1.4426950408889634
MASK_DIST = 1e30


def _alibi_slope(head_index, n_heads):
    return 2.0 ** (-8.0 * (head_index + 1) / n_heads)


def _nsa_prompt_kernel(q_ref, kc_ref, vc_ref, ks_ref, vs_ref, kw_ref, vw_ref, gl_ref, o_ref,
                       q_s, kc_s, vc_s, ks_s, vs_s, kw_s, vw_s, s_s, p_s, acc_s, m_s, l_s, a_s):
    g = pl.program_id(1)
    qi = pl.program_id(2)
    TQ, HD, R, L = NSA_TQ, NSA_HD, NSA_REP, NSA_BLOCK
    T = ks_ref.shape[1]
    nb = kc_ref.shape[1]
    scale = HD ** -0.5
    bf16 = jnp.bfloat16

    @pl.when(qi == 0)
    def _():
        ks_s[...] = ks_ref[0].astype(bf16)
        vs_s[...] = vs_ref[0].astype(bf16)
        kw_s[...] = kw_ref[0].astype(bf16)
        vw_s[...] = vw_ref[0].astype(bf16)
        kc_s[...] = jnp.zeros_like(kc_s)
        vc_s[...] = jnp.zeros_like(vc_s)
        kc_s[0:nb, :] = kc_ref[0].astype(bf16)
        vc_s[0:nb, :] = vc_ref[0].astype(bf16)

    for r in range(R):
        q_s[r * TQ:(r + 1) * TQ, :] = q_ref[0, :, r * HD:(r + 1) * HD].astype(bf16)

    gate = 1.0 / (1.0 + jnp.exp(-gl_ref[0, 0]))
    slopes = [jnp.where(g == 0, _alibi_slope(r, NSA_HEADS), _alibi_slope(R + r, NSA_HEADS)) for r in range(R)]
    qpos0 = qi * TQ

    row_c = lax.broadcasted_iota(jnp.int32, (TQ, 128), 0) + qpos0
    col_c = lax.broadcasted_iota(jnp.int32, (TQ, 128), 1)
    cur_c = lax.shift_right_logical(row_c, 6)
    mask_c = col_c < cur_c
    dist_c = (row_c - (col_c * L + (L - 1))).astype(jnp.float32)
    imp = jnp.zeros((TQ, 128), jnp.float32)
    s_s[:, 0:128] = _dot_nt(q_s[...], kc_s[...]) * scale
    for r in range(R):
        s = s_s[r * TQ:(r + 1) * TQ, 0:128] - slopes[r] * dist_c
        s = jnp.where(mask_c, s, NEG_INF)
        p = jnp.exp(s - jnp.max(s, axis=-1, keepdims=True))
        p = p / jnp.sum(p, axis=-1, keepdims=True)
        p = jnp.where(mask_c, p, 0.0)
        imp = imp + p
        o = jnp.dot(p.astype(bf16), vc_s[...], preferred_element_type=jnp.float32)
        o_ref[0, :, r * HD:(r + 1) * HD] = gate[:, r:r + 1] * o

    valid = col_c <= cur_c
    forced = valid & ((col_c == 0) | (col_c >= cur_c - 1))
    score = jnp.where(forced, jnp.inf, jnp.where(valid, imp, -jnp.inf))
    sc_t = jnp.transpose(score)[0:32, :]
    jrow = lax.broadcasted_iota(jnp.int32, (32, TQ), 0)
    rank = jnp.zeros((32, TQ), jnp.int32)
    for i in range(32):
        row = sc_t[i:i + 1, :]
        ahead = (row > sc_t) | ((row == sc_t) & (jrow > i))
        rank = rank + ahead.astype(jnp.int32)
    sel_t = jnp.where(rank < NSA_TOPN, 1.0, 0.0)
    sel_t = jnp.concatenate([sel_t, jnp.zeros((128 - 32, TQ), jnp.float32)], axis=0)
    sel = jnp.transpose(sel_t).astype(bf16)

    WK = NSA_WINDOW + TQ
    w0 = pl.multiple_of(jnp.maximum(qi - NSA_WINDOW // TQ, 0) * TQ, TQ)
    row_w = lax.broadcasted_iota(jnp.int32, (TQ, WK), 0) + qpos0
    rel_w = row_w - (lax.broadcasted_iota(jnp.int32, (TQ, WK), 1) + w0)
    ok_w = (rel_w >= 0) & (rel_w < NSA_WINDOW)
    relm_w = jnp.where(ok_w, rel_w.astype(jnp.float32), MASK_DIST)
    s_s[...] = _dot_nt(q_s[...], kw_s[pl.ds(w0, WK), :]) * (scale * LOG2E)
    for r in range(R):
        rows = slice(r * TQ, (r + 1) * TQ)
        s = s_s[rows, :] - (slopes[r] * LOG2E) * relm_w
        p = jnp.exp2(s - jnp.max(s, axis=-1, keepdims=True))
        l_s[rows, :] = jnp.sum(p, axis=-1, keepdims=True)
        p_s[rows, :] = p.astype(bf16)
    acc_s[...] = jnp.dot(p_s[...], vw_s[pl.ds(w0, WK), :], preferred_element_type=jnp.float32) / l_s[...]
    for r in range(R):
        rows = slice(r * TQ, (r + 1) * TQ)
        o_ref[0, :, r * HD:(r + 1) * HD] += gate[:, 2 * R + r:2 * R + r + 1] * acc_s[rows, :]

    m_s[...] = jnp.full_like(m_s, -jnp.inf)
    l_s[...] = jnp.zeros_like(l_s)
    acc_s[...] = jnp.zeros_like(acc_s)
    TK = NSA_TK
    n_chunks = (qpos0 + TQ + TK - 1) // TK

    def chunk(c, carry):
        k0 = pl.multiple_of(c * TK, TK)
        row_t = lax.broadcasted_iota(jnp.int32, (TQ, TK), 0) + qpos0
        rel = row_t - (lax.broadcasted_iota(jnp.int32, (TQ, TK), 1) + k0)
        blk = lax.shift_right_logical(lax.broadcasted_iota(jnp.int32, (128, TK), 1) + k0, 6)
        expand = jnp.where(lax.broadcasted_iota(jnp.int32, (128, TK), 0) == blk, 1.0, 0.0).astype(bf16)
        picked = jnp.dot(sel, expand, preferred_element_type=jnp.float32)
        ok = (picked > 0.5) & (rel >= 0)
        relm = jnp.where(ok, rel.astype(jnp.float32), MASK_DIST)
        s_s[:, 0:TK] = _dot_nt(q_s[...], ks_s[pl.ds(k0, TK), :]) * (scale * LOG2E)
        for r in range(R):
            rows = slice(r * TQ, (r + 1) * TQ)
            s = s_s[rows, 0:TK] - (slopes[r] * LOG2E) * relm
            m_old = m_s[rows, :]
            m_new = jnp.maximum(m_old, jnp.max(s, axis=-1, keepdims=True))
            alpha = jnp.exp2(m_old - m_new)
            p = jnp.exp2(s - m_new)
            l_s[rows, :] = alpha * l_s[rows, :] + jnp.sum(p, axis=-1, keepdims=True)
            p_s[rows, 0:TK] = p.astype(bf16)
            a_s[rows, :] = alpha
            m_s[rows, :] = m_new
        acc_s[...] = a_s[...] * acc_s[...] + jnp.dot(p_s[:, 0:TK], vs_s[pl.ds(k0, TK), :],
                                                     preferred_element_type=jnp.float32)
        return carry

    lax.fori_loop(0, n_chunks, chunk, 0)
    acc_s[...] = acc_s[...] / l_s[...]
    for r in range(R):
        rows = slice(r * TQ, (r + 1) * TQ)
        o_ref[0, :, r * HD:(r + 1) * HD] += gate[:, R + r:R + r + 1] * acc_s[rows, :]


def _nsa_prompt_attention(p, cmp, gate_logits):
    B, T, _ = p.shape
    nb = cmp.shape[1]
    HD, R, G, TQ = NSA_HD, NSA_REP, NSA_KV, NSA_TQ
    kv0 = NSA_HEADS
    lane_kv = lambda branch, c: (lambda b, g, i: (b, 0, kv0 + branch * 2 * G + g * 2 + c))
    bf16 = jnp.bfloat16
    return pl.pallas_call(
        _nsa_prompt_kernel,
        grid=(B, G, T // TQ),
        in_specs=[
            pl.BlockSpec((1, TQ, R * HD), lambda b, g, i: (b, i, g)),
            pl.BlockSpec((1, nb, HD), lambda b, g, i: (b, 0, 2 * g)),
            pl.BlockSpec((1, nb, HD), lambda b, g, i: (b, 0, 2 * g + 1)),
            pl.BlockSpec((1, T, HD), lane_kv(1, 0)),
            pl.BlockSpec((1, T, HD), lane_kv(1, 1)),
            pl.BlockSpec((1, T, HD), lane_kv(2, 0)),
            pl.BlockSpec((1, T, HD), lane_kv(2, 1)),
            pl.BlockSpec((1, 1, TQ, 3 * R), lambda b, g, i: (b, g, i, 0)),
        ],
        out_specs=pl.BlockSpec((1, TQ, R * HD), lambda b, g, i: (b, i, g)),
        out_shape=jax.ShapeDtypeStruct((B, T, NSA_HEADS * HD), jnp.float32),
        scratch_shapes=[
            pltpu.VMEM((R * TQ, HD), bf16),
            pltpu.VMEM((128, HD), bf16), pltpu.VMEM((128, HD), bf16),
            pltpu.VMEM((T, HD), bf16), pltpu.VMEM((T, HD), bf16),
            pltpu.VMEM((T, HD), bf16), pltpu.VMEM((T, HD), bf16),
            pltpu.VMEM((R * TQ, NSA_WINDOW + TQ), jnp.float32),
            pltpu.VMEM((R * TQ, NSA_WINDOW + TQ), bf16),
            pltpu.VMEM((R * TQ, HD), jnp.float32),
            pltpu.VMEM((R * TQ, 1), jnp.float32), pltpu.VMEM((R * TQ, 1), jnp.float32),
            pltpu.VMEM((R * TQ, 1), jnp.float32),
        ],
        compiler_params=pltpu.CompilerParams(dimension_semantics=("arbitrary", "arbitrary", "arbitrary"),
                                             vmem_limit_bytes=VMEM_LIMIT_BYTES),
        name="nsa_prompt_attention",
    )(p, cmp, cmp, p, p, p, p, gate_logits)


NSA_SEL_PAST = NSA_TOPN - 1
SEL_SLOTS = 1024
WIN_SLOTS = NSA_WINDOW + 128
NEW_ROWS = 16


def _row_slopes(g, n_rows):
    r = lax.broadcasted_iota(jnp.int32, (n_rows, 1), 0) % NSA_REP
    return jnp.exp2(-8.0 * (g * NSA_REP + r + 1).astype(jnp.float32) / NSA_HEADS)


def _nsa_sample_cmp_kernel(q_ref, kc_ref, vc_ref, oc_ref, sel_ref):
    g = pl.program_id(1)
    S = q_ref.shape[2] // NSA_REP
    nbp = kc_ref.shape[0]
    L = NSA_BLOCK
    n_rows = S * NSA_REP
    bf16 = jnp.bfloat16
    q = q_ref[0, 0].astype(bf16)
    row = lax.broadcasted_iota(jnp.int32, (n_rows, nbp), 0)
    col = lax.broadcasted_iota(jnp.int32, (n_rows, nbp), 1)
    qpos = PAST_LEN + row // NSA_REP
    cur = qpos // L
    mask = col < cur
    dist = (qpos - (col * L + (L - 1))).astype(jnp.float32)
    s = _dot_nt(q, kc_ref[...].astype(bf16)) * (NSA_HD ** -0.5) - _row_slopes(g, n_rows) * dist
    s = jnp.where(mask, s, NEG_INF)
    p = jnp.exp(s - jnp.max(s, axis=-1, keepdims=True))
    p = p / jnp.sum(p, axis=-1, keepdims=True)
    p = jnp.where(mask, p, 0.0)
    oc_ref[0, 0] = jnp.dot(p.astype(bf16), vc_ref[...].astype(bf16), preferred_element_type=jnp.float32)

    imp = jnp.sum(p.reshape(S, NSA_REP, nbp), axis=1)
    colS = lax.broadcasted_iota(jnp.int32, (S, nbp), 1)
    curS = (PAST_LEN + lax.broadcasted_iota(jnp.int32, (S, nbp), 0)) // L
    valid = colS < curS
    forced = valid & ((colS == 0) | (colS == curS - 1))
    score = jnp.where(forced, jnp.inf, jnp.where(valid, imp, -jnp.inf))
    score_t = jnp.transpose(jnp.concatenate([score, jnp.zeros((128 - S, nbp), jnp.float32)], axis=0))
    ii = lax.broadcasted_iota(jnp.int32, (nbp, nbp), 0)
    jj = lax.broadcasted_iota(jnp.int32, (nbp, nbp), 1)
    lane = lax.broadcasted_iota(jnp.int32, (1, 128), 1)
    jrow = lax.broadcasted_iota(jnp.int32, (1, nbp), 1).astype(jnp.float32)
    out_rows = []
    for si in range(S):
        c = score_t[:, si:si + 1]
        rw = score[si:si + 1, :]
        ahead = (c > rw) | ((c == rw) & (ii < jj))
        rank = jnp.sum(jnp.where(ahead, 1.0, 0.0), axis=0, keepdims=True)
        ids = jnp.zeros((1, 128), jnp.float32)
        for k in range(NSA_SEL_PAST):
            idx = jnp.sum(jnp.where(rank == k, jrow, 0.0), axis=1, keepdims=True)
            ids = jnp.where(lane == k, idx, ids)
        out_rows.append(ids.astype(jnp.int32))
    out_rows.append(jnp.zeros((8 - S, 128), jnp.int32))
    sel_ref[0, 0] = jnp.concatenate(out_rows, axis=0)


def _nsa_sample_cmp(q_rows, cmp_rows):
    Bd, G, n_rows, HD = q_rows.shape
    nbp = cmp_rows.shape[0] // Bd
    return pl.pallas_call(
        _nsa_sample_cmp_kernel,
        grid=(Bd, G),
        in_specs=[pl.BlockSpec((1, 1, n_rows, HD), lambda b, g: (b, g, 0, 0)),
                  pl.BlockSpec((nbp, HD), lambda b, g: (b, 2 * g)),
                  pl.BlockSpec((nbp, HD), lambda b, g: (b, 2 * g + 1))],
        out_specs=[pl.BlockSpec((1, 1, n_rows, HD), lambda b, g: (b, g, 0, 0)),
                   pl.BlockSpec((1, 1, 8, 128), lambda b, g: (b, g, 0, 0))],
        out_shape=[jax.ShapeDtypeStruct((Bd, G, n_rows, HD), jnp.float32),
                   jax.ShapeDtypeStruct((Bd, G, 8, 128), jnp.int32)],
        compiler_params=pltpu.CompilerParams(dimension_semantics=("arbitrary", "arbitrary"),
                                             vmem_limit_bytes=VMEM_LIMIT_BYTES),
        name="nsa_sample_cmp",
    )(q_rows, cmp_rows, cmp_rows)


def _nsa_sample_sel_kernel(phys_ref, sel_ref, q_ref, oc_ref, gate_ref, new_ref, win_ref, *rest):
    blk_refs = rest[:NSA_SEL_PAST]
    o_ref = rest[NSA_SEL_PAST]
    ks_s, vs_s, kw_s, vw_s = rest[NSA_SEL_PAST + 1:]
    b = pl.program_id(0)
    g = pl.program_id(1)
    s_idx = pl.program_id(2)
    S = pl.num_programs(2)
    R, L, HD = NSA_REP, NSA_BLOCK, NSA_HD
    bf16 = jnp.bfloat16
    n_past = NSA_SEL_PAST * L
    wb = win_ref.shape[0] // N_SLABS
    base = ((b * S + s_idx) * NSA_KV + g) * NSA_SEL_PAST
    scale = HD ** -0.5
    slopes = _row_slopes(g, R)
    qpos = PAST_LEN + s_idx

    @pl.when(s_idx == 0)
    def _():
        kw_s[...] = jnp.zeros_like(kw_s)
        vw_s[...] = jnp.zeros_like(vw_s)
        kw_s[0:wb, :] = win_ref[pl.ds(g * 2, wb, stride=N_SLABS), :].astype(bf16)
        vw_s[0:wb, :] = win_ref[pl.ds(g * 2 + 1, wb, stride=N_SLABS), :].astype(bf16)
        kw_s[wb:wb + NEW_ROWS, :] = new_ref[0, 0, 1, 0].astype(bf16)
        vw_s[wb:wb + NEW_ROWS, :] = new_ref[0, 0, 1, 1].astype(bf16)
        ks_s[n_past:, :] = jnp.zeros((SEL_SLOTS - n_past, HD), bf16)
        vs_s[n_past:, :] = jnp.zeros((SEL_SLOTS - n_past, HD), bf16)
        ks_s[n_past:n_past + NEW_ROWS, :] = new_ref[0, 0, 0, 0].astype(bf16)
        vs_s[n_past:n_past + NEW_ROWS, :] = new_ref[0, 0, 0, 1].astype(bf16)

    q = q_ref[0, 0].astype(bf16)
    gate = 1.0 / (1.0 + jnp.exp(-gate_ref[0, 0]))

    lane = lax.broadcasted_iota(jnp.int32, (1, SEL_SLOTS), 1)
    slot = lane // L
    kpos = jnp.where(lane >= n_past, PAST_LEN + (lane - n_past), lane % L)
    for k in range(NSA_SEL_PAST):
        ks_s[k * L:(k + 1) * L, :] = blk_refs[k][pl.ds(g * 2, L, stride=N_SLABS), :].astype(bf16)
        vs_s[k * L:(k + 1) * L, :] = blk_refs[k][pl.ds(g * 2 + 1, L, stride=N_SLABS), :].astype(bf16)
        kpos = kpos + jnp.where(slot == k, sel_ref[base + k] * L, 0)
    rel = qpos - kpos
    ok = (rel >= 0) & (lane < n_past + S)
    sc = _dot_nt(q, ks_s[...]) * scale - slopes * rel.astype(jnp.float32)
    sc = jnp.where(ok, sc, NEG_INF)
    p = jnp.exp(sc - jnp.max(sc, axis=-1, keepdims=True))
    p = p / jnp.sum(p, axis=-1, keepdims=True)
    p = jnp.where(ok, p, 0.0)
    o_s = jnp.dot(p.astype(bf16), vs_s[...], preferred_element_type=jnp.float32)

    lane_w = lax.broadcasted_iota(jnp.int32, (1, WIN_SLOTS), 1)
    kpos_w = jnp.where(lane_w >= wb, PAST_LEN + (lane_w - wb), PAST_LEN - wb + lane_w)
    rel_w = qpos - kpos_w
    ok_w = (rel_w >= 0) & (rel_w < NSA_WINDOW) & (lane_w < wb + S)
    sw = _dot_nt(q, kw_s[...]) * scale - slopes * rel_w.astype(jnp.float32)
    sw = jnp.where(ok_w, sw, NEG_INF)
    pw = jnp.exp(sw - jnp.max(sw, axis=-1, keepdims=True))
    pw = pw / jnp.sum(pw, axis=-1, keepdims=True)
    pw = jnp.where(ok_w, pw, 0.0)
    o_w = jnp.dot(pw.astype(bf16), vw_s[...], preferred_element_type=jnp.float32)

    o_ref[0, 0] = gate[:, 0:1] * oc_ref[0, 0] + gate[:, 1:2] * o_s + gate[:, 2:3] * o_w


def _nsa_sample_sel(phys_blk, sel_blk, q_rows, o_c, gate_rows, new_kv, win_rows, cache_blocks, layer):
    Bd, G, n_rows, HD = q_rows.shape
    S = n_rows // NSA_REP
    R = NSA_REP
    bf16 = jnp.bfloat16
    row_blk = lambda b, g, s, ph, se: (b, g, s, 0)

    def blk_spec(k):
        return pl.BlockSpec((None, None, SLAB_ROWS, HD),
                            lambda b, g, s, ph, se: (layer, ph[((b * S + s) * NSA_KV + g) * NSA_SEL_PAST + k], 0, 0))

    return pl.pallas_call(
        _nsa_sample_sel_kernel,
        grid_spec=pltpu.PrefetchScalarGridSpec(
            num_scalar_prefetch=2,
            grid=(Bd, G, S),
            in_specs=[pl.BlockSpec((1, 1, R, HD), row_blk),
                      pl.BlockSpec((1, 1, R, HD), row_blk),
                      pl.BlockSpec((1, 1, R, 3), row_blk),
                      pl.BlockSpec((1, 1, 2, 2, NEW_ROWS, HD), lambda b, g, s, ph, se: (b, g, 0, 0, 0, 0)),
                      pl.BlockSpec((None, None, win_rows.shape[2], HD), lambda b, g, s, ph, se: (layer, b, 0, 0))]
                     + [blk_spec(k) for k in range(NSA_SEL_PAST)],
            out_specs=pl.BlockSpec((1, 1, R, HD), row_blk),
            scratch_shapes=[pltpu.VMEM((SEL_SLOTS, HD), bf16), pltpu.VMEM((SEL_SLOTS, HD), bf16),
                            pltpu.VMEM((WIN_SLOTS, HD), bf16), pltpu.VMEM((WIN_SLOTS, HD), bf16)],
        ),
        out_shape=jax.ShapeDtypeStruct((Bd, G, n_rows, HD), jnp.float32),
        compiler_params=pltpu.CompilerParams(dimension_semantics=("arbitrary", "arbitrary", "arbitrary"),
                                             vmem_limit_bytes=VMEM_LIMIT_BYTES),
        name="nsa_sample_sel",
    )(phys_blk, sel_blk, q_rows, o_c, gate_rows, new_kv, win_rows, *([cache_blocks] * NSA_SEL_PAST))


GLA_C = 128
GLA_SUB = 16


def _gla_prompt_kernel(q_ref, k_ref, v_ref, g_ref, o_ref, st_ref, st_s, q_s, k_s, b_s, a_s):
    c = pl.program_id(2)
    C, SUB = GLA_C, GLA_SUB
    bf16 = jnp.bfloat16
    f32 = jnp.float32

    @pl.when(c == 0)
    def _():
        st_s[...] = jnp.zeros_like(st_s)

    q = q_ref[0] * (GLA_DK ** -0.5)
    k = k_ref[0]
    g = g_ref[0]
    vb = v_ref[0].astype(bf16)
    tri = jnp.where(lax.broadcasted_iota(jnp.int32, (C, C), 0) >= lax.broadcasted_iota(jnp.int32, (C, C), 1),
                    1.0, 0.0).astype(bf16)
    g_hi = g.astype(bf16)
    r1 = g - g_hi.astype(f32)
    g_mid = r1.astype(bf16)
    g_lo = (r1 - g_mid.astype(f32)).astype(bf16)
    b = (jnp.dot(tri, g_hi, preferred_element_type=f32) + jnp.dot(tri, g_mid, preferred_element_type=f32)
         + jnp.dot(tri, g_lo, preferred_element_type=f32))
    q_s[...] = q
    k_s[...] = k
    b_s[...] = b
    o = _dot_nt((q * jnp.exp(b)).astype(bf16), st_s[...].astype(bf16))

    lane = lax.broadcasted_iota(jnp.int32, (SUB, C), 1)
    trow = lax.broadcasted_iota(jnp.int32, (SUB, 1), 0)

    def sub_block(blk, carry):
        r0 = pl.multiple_of(blk * SUB, SUB)
        qi = q_s[pl.ds(r0, SUB), :]
        bi = b_s[pl.ds(r0, SUB), :]
        ref = jnp.where(blk > 0, b_s[pl.ds(jnp.maximum(r0 - 1, 0), 1), :], 0.0)
        qt = qi * jnp.exp(bi - ref)
        kt = k_s[...] * jnp.exp(jnp.minimum(ref - b_s[...], 0.0))
        a = jnp.where(lane < r0, _dot_nt(qt.astype(bf16), kt.astype(bf16)), 0.0)
        for s in range(SUB):
            ks = k_s[pl.ds(r0 + s, 1), :]
            bs = b_s[pl.ds(r0 + s, 1), :]
            col = jnp.sum(qi * jnp.exp(jnp.minimum(bi - bs, 0.0)) * ks, axis=-1, keepdims=True)
            a = jnp.where(lane == r0 + s, jnp.where(trow >= s, col, 0.0), a)
        a_s[pl.ds(r0, SUB), :] = a
        return carry

    lax.fori_loop(0, C // SUB, sub_block, 0)
    o_ref[0] = o + jnp.dot(a_s[...].astype(bf16), vb, preferred_element_type=f32)

    b_last = b[C - 1:C, :]
    kh = (k * jnp.exp(b_last - b)).astype(bf16)
    upd = lax.dot_general(vb, kh, (((0,), (0,)), ((), ())), preferred_element_type=f32)
    st_s[...] = st_s[...] * jnp.exp(b_last) + upd

    @pl.when(c == pl.num_programs(2) - 1)
    def _():
        st_ref[0, 0] = jnp.transpose(st_s[...])


def _gla_prompt(p, g):
    B, T, _ = p.shape
    H, DK, DV, C = GLA_HEADS, GLA_DK, GLA_DV, GLA_C
    f32 = jnp.float32
    return pl.pallas_call(
        _gla_prompt_kernel,
        grid=(B, H, T // C),
        in_specs=[pl.BlockSpec((1, C, DK), lambda b, h, c: (b, c, h)),
                  pl.BlockSpec((1, C, DK), lambda b, h, c: (b, c, H + h)),
                  pl.BlockSpec((1, C, DV), lambda b, h, c: (b, c, 2 * H * DK // DV + h)),
                  pl.BlockSpec((1, C, DK), lambda b, h, c: (b, c, h))],
        out_specs=[pl.BlockSpec((1, C, DV), lambda b, h, c: (b, c, h)),
                   pl.BlockSpec((1, 1, DK, DV), lambda b, h, c: (b, h, 0, 0))],
        out_shape=[jax.ShapeDtypeStruct((B, T, H * DV), f32), jax.ShapeDtypeStruct((B, H, DK, DV), f32)],
        scratch_shapes=[pltpu.VMEM((DV, DK), f32), pltpu.VMEM((C, DK), f32), pltpu.VMEM((C, DK), f32),
                        pltpu.VMEM((C, DK), f32), pltpu.VMEM((C, C), f32)],
        compiler_params=pltpu.CompilerParams(dimension_semantics=("arbitrary", "arbitrary", "arbitrary"),
                                             vmem_limit_bytes=VMEM_LIMIT_BYTES),
        name="gla_prompt",
    )(p, p, p, g)


def _rmsnorm(x, g):
    xf = x.astype(jnp.float32)
    y = xf * lax.rsqrt(jnp.mean(xf * xf, axis=-1, keepdims=True) + EPS)
    return y.astype(x.dtype) * g


def _alibi_slopes(n_heads, n_groups):
    h = jnp.arange(1, n_heads + 1, dtype=jnp.float32)
    return jnp.exp2(-8.0 * h / n_heads).reshape(n_groups, n_heads // n_groups)


def _adaln(c, w, b):
    m = jax.nn.silu(c) @ w + b
    return m.reshape(c.shape[0], 6, 1, D_MODEL)


def _attend(q, k, v, mask, dist, slopes, sink=None):
    s = jnp.einsum('...qgrd,...kgd->...qgrk', q, k).astype(jnp.float32) * (q.shape[-1] ** -0.5)
    s = s - slopes[:, :, None] * dist[..., :, None, None, :]
    m = mask[..., :, None, None, :]
    s = jnp.where(m, s, NEG_INF)
    if sink is not None:
        sk = jnp.broadcast_to(sink.astype(jnp.float32)[:, :, None], s.shape[:-1] + (1,))
        p = jax.nn.softmax(jnp.concatenate([s, sk], axis=-1), axis=-1)[..., :-1]
    else:
        p = jax.nn.softmax(s, axis=-1)
    p = jnp.where(m, p, 0.0)
    o = jnp.einsum('...qgrk,...kgd->...qgrd', p.astype(v.dtype), v)
    return o, p


def _window_prompt(q, kv, window, slopes, sink=None):
    B, T = q.shape[:2]
    nqb = T // QBLOCK
    n_prev = -(-(window - 1) // QBLOCK)
    kb = (n_prev + 1) * QBLOCK
    pad = jnp.pad(kv, [(0, 0), (n_prev * QBLOCK, 0)] + [(0, 0)] * (kv.ndim - 2))
    blocks = pad.reshape((B, nqb + n_prev, QBLOCK) + kv.shape[2:])
    band = jnp.concatenate([blocks[:, j:j + nqb] for j in range(n_prev + 1)], axis=2)
    qpos = jnp.arange(T).reshape(nqb, QBLOCK)
    kpos = (jnp.arange(nqb)[:, None] - n_prev) * QBLOCK + jnp.arange(kb)[None, :]
    rel = qpos[:, :, None] - kpos[:, None, :]
    mask = (kpos[:, None, :] >= 0) & (rel >= 0) & (rel < window)
    qb = q.reshape((B, nqb, QBLOCK) + q.shape[2:])
    o, _ = _attend(qb, band[..., 0, :], band[..., 1, :], mask, rel.astype(jnp.float32), slopes, sink)
    return o.reshape(q.shape)


def _window_sample(q, kv_new, buf, window, slopes, sink=None):
    S = q.shape[1]
    wb = buf.shape[1]
    kv = jnp.concatenate([buf.astype(kv_new.dtype), kv_new], axis=1)
    qpos = PAST_LEN + jnp.arange(S)
    kpos = PAST_LEN - wb + jnp.arange(wb + S)
    rel = qpos[:, None] - kpos[None, :]
    mask = (rel >= 0) & (rel < window)
    o, _ = _attend(q, kv[..., 0, :], kv[..., 1, :], mask, rel.astype(jnp.float32), slopes, sink)
    return o, kv[:, S:]


def _sel_attend(q, k, v, kpos, qpos, slopes):
    s = jnp.einsum('bqgrd,bqgkd->bqgrk', q, k).astype(jnp.float32) * (q.shape[-1] ** -0.5)
    rel = qpos[None, :, None, None] - kpos
    s = s - slopes[None, None, :, :, None] * rel[:, :, :, None, :].astype(jnp.float32)
    m = (rel >= 0)[:, :, :, None, :]
    p = jnp.where(m, jax.nn.softmax(jnp.where(m, s, NEG_INF), axis=-1), 0.0)
    return jnp.einsum('bqgrk,bqgkd->bqgrd', p.astype(v.dtype), v)


def _nsa_project(h, w_in):
    B, T = h.shape[:2]
    p = h @ w_in
    nq = NSA_HEADS * NSA_HD
    nkv = 2 * NSA_KV * NSA_HD
    q = p[..., :nq].reshape(B, T, NSA_KV, NSA_REP, NSA_HD)
    kv = p[..., nq:nq + 3 * nkv].reshape(B, T, 3, NSA_KV, 2, NSA_HD)
    gates = jax.nn.sigmoid(p[..., nq + 3 * nkv:].reshape(B, T, 3, NSA_KV, NSA_REP))
    return q, kv[:, :, 0], kv[:, :, 1], kv[:, :, 2], gates


def _nsa_compress(blocks, pe, w1, w2):
    x = blocks + pe[:, None]
    hid = jax.nn.silu(jnp.einsum('...nlgcd,cldh->...ngch', x, w1))
    return jnp.einsum('...ngch,chd->...ngcd', hid, w2)


def _nsa_cmp_attend(q, cmp, qpos, slopes):
    nb = cmp.shape[-4]
    j = jnp.arange(nb)
    mask = j[None, :] < (qpos // NSA_BLOCK)[:, None]
    dist = (qpos[:, None] - (j[None, :] * NSA_BLOCK + NSA_BLOCK - 1)).astype(jnp.float32)
    o, p = _attend(q, cmp[..., 0, :], cmp[..., 1, :], mask, dist, slopes)
    return o, p.sum(axis=-2)


def _nsa_merge(gates, o_c, o_s, o_w, w_out):
    o = gates[:, :, 0, :, :, None] * o_c + gates[:, :, 1, :, :, None] * o_s + gates[:, :, 2, :, :, None] * o_w
    return o.reshape(o.shape[0], o.shape[1], -1) @ w_out


def _nsa_layer(hp, hs, cache_cmp, cache_slc, win_all, page_table, li, w_in, pe, w1, w2, w_out):
    L = NSA_BLOCK
    B, T = hp.shape[:2]
    nq = NSA_HEADS * NSA_HD
    nkv = 2 * NSA_KV * NSA_HD
    kv_shape = (NSA_KV, 2, NSA_HD)
    pe2, w1bd, w2bd = _compress_weights(pe, w1, w2)
    pp = hp @ w_in
    kvc_rows = pp[..., nq:nq + nkv]
    kvc = kvc_rows.reshape((B, T) + kv_shape)
    kvs = pp[..., nq + nkv:nq + 2 * nkv].reshape((B, T) + kv_shape)
    win_p = pp[:, T - min(NSA_WINDOW, T):, nq + 2 * nkv:nq + 3 * nkv].reshape((B, min(NSA_WINDOW, T)) + kv_shape)
    cmp_p = _nsa_compress_dense(kvc_rows.reshape(B * T * N_SLABS, NSA_HD), pe2, w1bd, w2bd).reshape(B, T // L, nkv)
    gl = pp[..., nq + 3 * nkv:].reshape(B, T, 3, NSA_KV, NSA_REP).transpose(0, 3, 1, 2, 4)
    o_p = _nsa_prompt_attention(pp, cmp_p, gl.reshape(B, NSA_KV, T, 3 * NSA_REP))
    yp = o_p @ w_out

    Bd, S = hs.shape[:2]
    G, R, HD = NSA_KV, NSA_REP, NSA_HD
    ps = hs @ w_in
    kvc_s = ps[..., nq:nq + nkv].reshape((Bd, S) + kv_shape)
    kvs_s = ps[..., nq + nkv:nq + 2 * nkv].reshape((Bd, S) + kv_shape)
    kvw_s = ps[..., nq + 2 * nkv:nq + 3 * nkv].reshape((Bd, S) + kv_shape)
    q_rows = ps[..., :nq].reshape(Bd, S, G, R, HD).transpose(0, 2, 1, 3, 4).reshape(Bd, G, S * R, HD)
    gate_rows = ps[..., nq + 3 * nkv:].reshape(Bd, S, 3, G, R).transpose(0, 3, 1, 4, 2).reshape(Bd, G, S * R, 3)
    new_kv = ps[..., nq + nkv:nq + 3 * nkv].reshape(Bd, S, 2, G, 2, HD).transpose(0, 3, 2, 4, 1, 5)
    new_kv = jnp.pad(new_kv, [(0, 0)] * 4 + [(0, NEW_ROWS - S), (0, 0)])
    cache_rows = cache_cmp.reshape(cache_cmp.shape[:2] + (PAGE_SIZE * N_SLABS, HD))
    cmp = _nsa_compress_paged(cache_rows, li, page_table.reshape(-1), pe2, w1bd, w2bd)
    o_c, sel_out = _nsa_sample_cmp(q_rows, cmp)
    sel = sel_out[:, :, :S, :NSA_SEL_PAST].transpose(0, 2, 1, 3)
    bpp = PAGE_SIZE // L
    phys = page_table[jnp.arange(Bd)[:, None, None, None], sel // bpp] * bpp + sel % bpp
    cache_blocks = cache_slc.reshape(cache_slc.shape[0], cache_slc.shape[1] * bpp, SLAB_ROWS, HD)
    win_rows = win_all.reshape(win_all.shape[:2] + (win_all.shape[2] * N_SLABS, HD))
    o_s = _nsa_sample_sel(phys.reshape(-1), sel.reshape(-1), q_rows, o_c, gate_rows, new_kv, win_rows,
                          cache_blocks, li)
    ys = o_s.reshape(Bd, G, S, R, HD).transpose(0, 2, 1, 3, 4).reshape(Bd, S, nq) @ w_out
    win_s = jnp.concatenate([win_all[li], kvw_s], axis=1)[:, S:]
    return yp, ys, (kvc, kvc_s, kvs, kvs_s, win_p, win_s)


def _swa_project(h, w_in, b_in):
    B, T = h.shape[:2]
    p = h @ w_in + b_in
    nq = SWA_HEADS * SWA_HD
    q = p[..., :nq].reshape(B, T, SWA_KV, SWA_REP, SWA_HD)
    kv = p[..., nq:].reshape(B, T, SWA_KV, 2, SWA_HD)
    return q, kv


def _swa_layer(hp, hs, buf, w_in, b_in, sinks, w_out, b_out, slopes):
    sink = sinks.reshape(SWA_KV, SWA_REP)
    B, T = hp.shape[:2]
    q, kv = _swa_project(hp, w_in, b_in)
    o = _window_prompt(q, kv, SWA_WINDOW, slopes, sink)
    yp = o.reshape(B, T, -1) @ w_out + b_out
    buf_p = kv[:, T - min(SWA_WINDOW, T):]
    Bd, S = hs.shape[:2]
    q, kv_s = _swa_project(hs, w_in, b_in)
    o, buf_s = _window_sample(q, kv_s, buf, SWA_WINDOW, slopes, sink)
    ys = o.reshape(Bd, S, -1) @ w_out + b_out
    return yp, ys, (buf_p, buf_s)


def _gla_project(h, w_in, w_a2, b_a):
    B, T = h.shape[:2]
    p = h @ w_in
    nk = GLA_HEADS * GLA_DK
    nv = GLA_HEADS * GLA_DV
    q = p[..., :nk].reshape(B, T, GLA_HEADS, GLA_DK).astype(jnp.float32) * (GLA_DK ** -0.5)
    k = p[..., nk:2 * nk].reshape(B, T, GLA_HEADS, GLA_DK).astype(jnp.float32)
    v = p[..., 2 * nk:2 * nk + nv].reshape(B, T, GLA_HEADS, GLA_DV).astype(jnp.float32)
    r = p[..., 2 * nk + nv:2 * nk + 2 * nv].reshape(B, T, GLA_HEADS, GLA_DV)
    a = (p[..., 2 * nk + 2 * nv:] @ w_a2 + b_a).astype(jnp.float32)
    g = (jax.nn.log_sigmoid(a) / GLA_NORMALIZER).reshape(B, T, GLA_HEADS, GLA_DK)
    return q, k, v, g, r


def _gla_chunk(state, q, k, v, g):
    C = q.shape[1]
    b = jnp.cumsum(g, axis=1)
    causal = jnp.tril(jnp.ones((C, C), dtype=bool))
    o_inter = jnp.einsum('bthk,bhkv->bthv', q * jnp.exp(b), state)
    diff = b[:, :, None] - b[:, None, :]
    decay = jnp.exp(jnp.where(causal[None, :, :, None, None], diff, -jnp.inf))
    a = jnp.einsum('bthk,btshk,bshk->btsh', q, decay, k)
    o_intra = jnp.einsum('btsh,bshv->bthv', a, v)
    b_last = b[:, -1]
    new_state = jnp.exp(b_last)[..., None] * state + jnp.einsum('bshk,bshv->bhkv', k * jnp.exp(b_last[:, None] - b), v)
    return new_state, o_inter + o_intra


def _gla_layer(hp, hs, state, w_in, w_a2, b_a, norm, w_out):
    def readout(o, r, h):
        y = _rmsnorm(o, norm).astype(h.dtype) * jax.nn.silu(r)
        return y.reshape(h.shape[0], h.shape[1], -1) @ w_out

    B, T = hp.shape[:2]
    nk = GLA_HEADS * GLA_DK
    nv = GLA_HEADS * GLA_DV
    pp = hp @ w_in
    a = (pp[..., 2 * nk + 2 * nv:] @ w_a2 + b_a).astype(jnp.float32)
    g = jax.nn.log_sigmoid(a) / GLA_NORMALIZER
    o, s_p = _gla_prompt(pp, g)
    r = pp[..., 2 * nk + nv:2 * nk + 2 * nv].reshape(B, T, GLA_HEADS, GLA_DV)
    yp = readout(o.reshape(B, T, GLA_HEADS, GLA_DV), r, hp)
    q, k, v, g, r = _gla_project(hs, w_in, w_a2, b_a)
    s_s, o = _gla_chunk(state.astype(jnp.float32), q, k, v, g)
    ys = readout(o, r, hs)
    return yp, ys, (s_p.astype(hp.dtype), s_s.astype(state.dtype))


def kernel(x_prompt, x_sample, cache_nsa_cmp, cache_nsa_slc, state_nsa_win, state_swa_kv, state_gla,
           page_table, c_prompt, c_sample, ada_w, ada_b, norm_mix, norm_ffn, norm_final,
           nsa_w_in, nsa_cmp_pe, nsa_cmp_w1, nsa_cmp_w2, nsa_w_out,
           swa_w_in, swa_b_in, swa_sinks, swa_w_out, swa_b_out,
           gla_w_in, gla_w_a2, gla_b_a, gla_norm, gla_w_out,
           moe_w_router, moe_b_router, moe_w1, moe_b1, moe_w2, moe_b2):
    B, T = x_prompt.shape[:2]
    Bd, S = x_sample.shape[:2]
    n_p = B * T
    nsa_slopes = _alibi_slopes(NSA_HEADS, NSA_KV)
    swa_slopes = _alibi_slopes(SWA_HEADS, SWA_KV)
    xp, xs = x_prompt, x_sample
    nsa_new, swa_new, gla_new = [], [], []
    for i in range(DEPTH):
        mp = _adaln(c_prompt, ada_w[i], ada_b[i])
        ms = _adaln(c_sample, ada_w[i], ada_b[i])
        hp = _rmsnorm(xp, norm_mix[i]) * (1.0 + mp[:, 1]) + mp[:, 0]
        hs = _rmsnorm(xs, norm_mix[i]) * (1.0 + ms[:, 1]) + ms[:, 0]
        kind, li = i % N_MIXERS, i // N_MIXERS
        if kind == 0:
            yp, ys, st = _nsa_layer(hp, hs, cache_nsa_cmp, cache_nsa_slc, state_nsa_win, page_table, li,
                                    nsa_w_in[li], nsa_cmp_pe[li], nsa_cmp_w1[li], nsa_cmp_w2[li], nsa_w_out[li])
            nsa_new.append(st)
        elif kind == 1:
            yp, ys, st = _swa_layer(hp, hs, state_swa_kv[li], swa_w_in[li], swa_b_in[li], swa_sinks[li],
                                    swa_w_out[li], swa_b_out[li], swa_slopes)
            swa_new.append(st)
        else:
            yp, ys, st = _gla_layer(hp, hs, state_gla[li], gla_w_in[li], gla_w_a2[li], gla_b_a[li],
                                    gla_norm[li], gla_w_out[li])
            gla_new.append(st)
        xp = xp + mp[:, 2] * yp
        xs = xs + ms[:, 2] * ys
        hp = _rmsnorm(xp, norm_ffn[i]) * (1.0 + mp[:, 4]) + mp[:, 3]
        hs = _rmsnorm(xs, norm_ffn[i]) * (1.0 + ms[:, 4]) + ms[:, 3]
        h_all = jnp.concatenate([hp.reshape(n_p, D_MODEL), hs.reshape(Bd * S, D_MODEL)], axis=0)
        y_all = _moe(h_all, moe_w_router[i], moe_b_router[i], i, moe_w1, moe_b1, moe_w2, moe_b2)
        xp = xp + mp[:, 5] * y_all[:n_p].reshape(B, T, D_MODEL)
        xs = xs + ms[:, 5] * y_all[n_p:].reshape(Bd, S, D_MODEL)
    y_prompt = _rmsnorm(xp, norm_final)
    y_sample = _rmsnorm(xs, norm_final)
    stack = lambda sts, k: jnp.stack([st[k] for st in sts])
    return (y_prompt, y_sample, stack(nsa_new, 0), stack(nsa_new, 1), stack(nsa_new, 2), stack(nsa_new, 3),
            stack(nsa_new, 4), stack(nsa_new, 5), stack(swa_new, 0), stack(swa_new, 1),
            stack(gla_new, 0), stack(gla_new, 1))
```

```python
import functools

import jax
import jax.numpy as jnp
from jax import lax
from jax.experimental import pallas as pl
from jax.experimental.pallas import tpu as pltpu

D_MODEL = 2048
DEPTH = 4
PAST_LEN = 16384
PAGE_SIZE = 128
N_MIXERS = 3

NSA_HEADS = 16
NSA_KV = 2
NSA_HD = D_MODEL // NSA_HEADS
NSA_REP = NSA_HEADS // NSA_KV
NSA_BLOCK = 64
NSA_TOPN = 16
NSA_WINDOW = 512
NSA_SEL_QB = 64
SWA_HEADS = 32
SWA_KV = 4
SWA_HD = D_MODEL // SWA_HEADS
SWA_REP = SWA_HEADS // SWA_KV
SWA_WINDOW = 128
GLA_HEADS = 4
GLA_DK = D_MODEL // 2 // GLA_HEADS
GLA_DV = D_MODEL // GLA_HEADS
GLA_NORMALIZER = 16.0
GLA_CHUNK = 64
N_EXPERTS = 32
TOP_K = 4
D_FF = D_MODEL
SWIGLU_LIMIT = 7.0
SWIGLU_ALPHA = 1.702
QBLOCK = 128
NEG_INF = -1e30
EPS = 1e-6

VMEM_LIMIT_BYTES = 56 * 1024 * 1024
MOE_TM = 256
MOE_TF = 1024
MOE_TN = 1024
CAST_ROWS = 256


def _cast_rows(src_ref, dst_ref):
    n = dst_ref.shape[0] // CAST_ROWS

    def body(c, carry):
        r = pl.multiple_of(c * CAST_ROWS, CAST_ROWS)
        dst_ref[pl.ds(r, CAST_ROWS), :] = src_ref[pl.ds(r, CAST_ROWS), :].astype(jnp.bfloat16)
        return carry

    lax.fori_loop(0, n, body, 0)


def _stream_slab(j, i, n_j, meta, slab_copies, convert):
    te_ref, tf_ref, _, nx_ref, lg_ref, gi_ref, ng_ref = meta

    @pl.when(tf_ref[i] == 1)
    def _():
        slot = lax.rem(j * ng_ref[0] + gi_ref[i], 2)

        @pl.when((j == 0) & (i == 0))
        def _():
            for cp in slab_copies(te_ref[0], 0, slot):
                cp.start()

        for cp in slab_copies(te_ref[i], j, slot):
            cp.wait()
        convert(slot)
        nxt_j = j + lg_ref[i]

        @pl.when(nxt_j < n_j)
        def _():
            for cp in slab_copies(nx_ref[i], nxt_j, 1 - slot):
                cp.start()


def _moe_up_kernel(*refs, layer):
    meta, (x_ref, w1_ref, bg_ref, bu_ref, h_ref, wbuf, wg_s, wu_s, sem) = refs[:7], refs[7:]
    tv_ref = meta[2]
    j = pl.program_id(0)
    i = pl.program_id(1)

    def slab_copies(e, jj, slot):
        col = pl.multiple_of(jj * MOE_TF, MOE_TF)
        return [pltpu.make_async_copy(w1_ref.at[layer, e, :, pl.ds(c0 + col, MOE_TF)], wbuf.at[slot, c], sem.at[slot])
                for c, c0 in enumerate((0, D_FF))]

    def convert(slot):
        _cast_rows(wbuf.at[slot, 0], wg_s)
        _cast_rows(wbuf.at[slot, 1], wu_s)

    _stream_slab(j, i, pl.num_programs(0), meta, slab_copies, convert)

    @pl.when(tv_ref[i] == 1)
    def _():
        x = x_ref[...].astype(jnp.bfloat16)
        g = jnp.dot(x, wg_s[...], preferred_element_type=jnp.float32) + bg_ref[...]
        u = jnp.dot(x, wu_s[...], preferred_element_type=jnp.float32) + bu_ref[...]
        g = jnp.minimum(g, SWIGLU_LIMIT)
        u = jnp.clip(u, -SWIGLU_LIMIT, SWIGLU_LIMIT)
        sig = 1.0 / (1.0 + jnp.exp(-SWIGLU_ALPHA * g))
        h_ref[...] = (g * sig * (u + 1.0)).astype(h_ref.dtype)

    @pl.when(tv_ref[i] == 0)
    def _():
        h_ref[...] = jnp.zeros_like(h_ref)


def _moe_down_kernel(*refs, layer):
    meta, (h_ref, w2_ref, b_ref, y_ref, wbuf, w_s, sem) = refs[:7], refs[7:]
    tv_ref = meta[2]
    j = pl.program_id(0)
    i = pl.program_id(1)

    def slab_copies(e, jj, slot):
        col = pl.multiple_of(jj * MOE_TN, MOE_TN)
        return [pltpu.make_async_copy(w2_ref.at[layer, e, :, pl.ds(col, MOE_TN)], wbuf.at[slot], sem.at[slot])]

    _stream_slab(j, i, pl.num_programs(0), meta, slab_copies, lambda slot: _cast_rows(wbuf.at[slot], w_s))

    @pl.when(tv_ref[i] == 1)
    def _():
        y_ref[...] = jnp.dot(h_ref[...], w_s[...], preferred_element_type=jnp.float32) + b_ref[...]

    @pl.when(tv_ref[i] == 0)
    def _():
        y_ref[...] = jnp.zeros_like(y_ref)


def _moe_ffn(x_sorted, meta, layer, w1, b1, w2, b2):
    P, D = x_sorted.shape
    n_tiles = P // MOE_TM
    nj1 = D_FF // MOE_TF
    b1r = b1.reshape(DEPTH, N_EXPERTS, 1, 2 * D_FF)
    b2r = b2.reshape(DEPTH, N_EXPERTS, 1, D)
    params = pltpu.CompilerParams(dimension_semantics=("arbitrary", "arbitrary"),
                                  vmem_limit_bytes=VMEM_LIMIT_BYTES)
    bf16 = jnp.bfloat16
    h = pl.pallas_call(
        functools.partial(_moe_up_kernel, layer=layer),
        grid_spec=pltpu.PrefetchScalarGridSpec(
            num_scalar_prefetch=7,
            grid=(nj1, n_tiles),
            in_specs=[
                pl.BlockSpec((MOE_TM, D), lambda j, i, *m: (i, 0)),
                pl.BlockSpec(memory_space=pl.ANY),
                pl.BlockSpec((None, None, 1, MOE_TF), lambda j, i, te, *m: (layer, te[i], 0, j)),
                pl.BlockSpec((None, None, 1, MOE_TF), lambda j, i, te, *m: (layer, te[i], 0, nj1 + j)),
            ],
            out_specs=pl.BlockSpec((MOE_TM, MOE_TF), lambda j, i, *m: (i, j)),
            scratch_shapes=[pltpu.VMEM((2, 2, D, MOE_TF), jnp.float32),
                            pltpu.VMEM((D, MOE_TF), bf16), pltpu.VMEM((D, MOE_TF), bf16),
                            pltpu.SemaphoreType.DMA((2,))],
        ),
        out_shape=jax.ShapeDtypeStruct((P, D_FF), bf16),
        compiler_params=params,
        name="moe_up",
    )(*meta, x_sorted, w1, b1r, b1r)
    y = pl.pallas_call(
        functools.partial(_moe_down_kernel, layer=layer),
        grid_spec=pltpu.PrefetchScalarGridSpec(
            num_scalar_prefetch=7,
            grid=(D // MOE_TN, n_tiles),
            in_specs=[
                pl.BlockSpec((MOE_TM, D_FF), lambda j, i, *m: (i, 0)),
                pl.BlockSpec(memory_space=pl.ANY),
                pl.BlockSpec((None, None, 1, MOE_TN), lambda j, i, te, *m: (layer, te[i], 0, j)),
            ],
            out_specs=pl.BlockSpec((MOE_TM, MOE_TN), lambda j, i, *m: (i, j)),
            scratch_shapes=[pltpu.VMEM((2, D_FF, MOE_TN), jnp.float32), pltpu.VMEM((D_FF, MOE_TN), bf16),
                            pltpu.SemaphoreType.DMA((2,))],
        ),
        out_shape=jax.ShapeDtypeStruct((P, D), jnp.float32),
        compiler_params=params,
        name="moe_down",
    )(*meta, h, w2, b2r)
    return y


COMBINE_TOKENS = 64
ROW_DMA_UNROLL = 16


def _row_copy(src, src_row, dst, dst_row, sem):
    return pltpu.make_async_copy(src.at[pl.ds(src_row, 1), :], dst.at[pl.ds(dst_row, 1), :], sem)


def _moe_dispatch_kernel(tok_ref, x_ref, out_ref, buf, sem):
    i = pl.program_id(0)
    n_steps = pl.num_programs(0)
    slot = lax.rem(i, 2)

    def for_rows(step, slot_, act):
        def body(r, carry):
            act(_row_copy(x_ref, tok_ref[step * MOE_TM + r], buf.at[slot_], r, sem.at[slot_]))
            return carry
        lax.fori_loop(0, MOE_TM, body, 0, unroll=ROW_DMA_UNROLL)

    @pl.when(i == 0)
    def _():
        for_rows(0, 0, lambda cp: cp.start())

    @pl.when(i + 1 < n_steps)
    def _():
        for_rows(i + 1, 1 - slot, lambda cp: cp.start())

    for_rows(i, slot, lambda cp: cp.wait())
    out_ref[...] = buf[slot].astype(out_ref.dtype)


def _moe_dispatch(x, row_tok):
    D = x.shape[1]
    P = row_tok.shape[0]
    return pl.pallas_call(
        _moe_dispatch_kernel,
        grid_spec=pltpu.PrefetchScalarGridSpec(
            num_scalar_prefetch=1,
            grid=(P // MOE_TM,),
            in_specs=[pl.BlockSpec(memory_space=pl.ANY)],
            out_specs=pl.BlockSpec((MOE_TM, D), lambda i, t: (i, 0)),
            scratch_shapes=[pltpu.VMEM((2, MOE_TM, D), x.dtype), pltpu.SemaphoreType.DMA((2,))],
        ),
        out_shape=jax.ShapeDtypeStruct((P, D), jnp.bfloat16),
        compiler_params=pltpu.CompilerParams(dimension_semantics=("arbitrary",),
                                             vmem_limit_bytes=VMEM_LIMIT_BYTES),
        name="moe_dispatch",
    )(row_tok, x)


def _moe_combine_kernel(dest_ref, y_ref, gate_ref, out_ref, buf, sem):
    i = pl.program_id(0)
    n_steps = pl.num_programs(0)
    slot = lax.rem(i, 2)
    CT = COMBINE_TOKENS

    def for_rows(step, slot_, act):
        def body(t, carry):
            for k in range(TOP_K):
                a = (step * CT + t) * TOP_K + k
                act(_row_copy(y_ref, dest_ref[a], buf.at[slot_], k * CT + t, sem.at[slot_]))
            return carry
        lax.fori_loop(0, CT, body, 0, unroll=ROW_DMA_UNROLL // TOP_K)

    @pl.when(i == 0)
    def _():
        for_rows(0, 0, lambda cp: cp.start())

    @pl.when(i + 1 < n_steps)
    def _():
        for_rows(i + 1, 1 - slot, lambda cp: cp.start())

    for_rows(i, slot, lambda cp: cp.wait())
    gate = gate_ref[...]
    acc = buf[slot, 0:CT, :] * gate[:, 0:1]
    for k in range(1, TOP_K):
        acc = acc + buf[slot, k * CT:(k + 1) * CT, :] * gate[:, k:k + 1]
    out_ref[...] = acc


def _moe_combine(y, dest, gate):
    D = y.shape[1]
    N = gate.shape[0]
    CT = COMBINE_TOKENS
    return pl.pallas_call(
        _moe_combine_kernel,
        grid_spec=pltpu.PrefetchScalarGridSpec(
            num_scalar_prefetch=1,
            grid=(N // CT,),
            in_specs=[pl.BlockSpec(memory_space=pl.ANY), pl.BlockSpec((CT, TOP_K), lambda i, d: (i, 0))],
            out_specs=pl.BlockSpec((CT, D), lambda i, d: (i, 0)),
            scratch_shapes=[pltpu.VMEM((2, CT * TOP_K, D), jnp.float32), pltpu.SemaphoreType.DMA((2,))],
        ),
        out_shape=jax.ShapeDtypeStruct((N, D), jnp.float32),
        compiler_params=pltpu.CompilerParams(dimension_semantics=("arbitrary",),
                                             vmem_limit_bytes=VMEM_LIMIT_BYTES),
        name="moe_combine",
    )(dest, y, gate)


def _moe(x, w_r, b_r, layer, w1, b1, w2, b2):
    N, D = x.shape
    logits = (x @ w_r + b_r).astype(jnp.float32)
    top_v, top_i = lax.top_k(logits, TOP_K)
    gate = jax.nn.softmax(top_v, axis=-1)
    A = N * TOP_K
    e_flat = top_i.reshape(-1).astype(jnp.int32)
    onehot = (e_flat[:, None] == jnp.arange(N_EXPERTS, dtype=jnp.int32)[None, :]).astype(jnp.int32)
    csum = jnp.cumsum(onehot, axis=0)
    counts = csum[-1]
    rank = jnp.take_along_axis(csum, e_flat[:, None], axis=1)[:, 0] - 1
    padded = (counts + MOE_TM - 1) // MOE_TM * MOE_TM
    pend = jnp.cumsum(padded)
    pstart = pend - padded
    dest = pstart[e_flat] + rank
    n_tiles = -(-(A + N_EXPERTS * (MOE_TM - 1)) // MOE_TM)
    P = n_tiles * MOE_TM
    tile_start = jnp.arange(n_tiles, dtype=jnp.int32) * MOE_TM
    tile_valid = tile_start < pend[-1]
    tile_e = jnp.minimum(jnp.searchsorted(pend, tile_start, side='right'), N_EXPERTS - 1).astype(jnp.int32)
    prev_e = jnp.concatenate([jnp.full((1,), -1, jnp.int32), tile_e[:-1]])
    tile_first = ((tile_e != prev_e) & tile_valid).astype(jnp.int32)
    group_idx = jnp.cumsum(tile_first) - 1
    n_groups = jnp.sum(tile_first)
    group_e = jnp.argsort(counts == 0, stable=True).astype(jnp.int32)
    next_e = group_e[(group_idx + 1) % n_groups]
    last_group = (group_idx + 1 == n_groups).astype(jnp.int32)
    meta = (tile_e, tile_first, tile_valid.astype(jnp.int32), next_e, last_group, group_idx.astype(jnp.int32),
            n_groups.reshape(1).astype(jnp.int32))
    tok_flat = jnp.repeat(jnp.arange(N, dtype=jnp.int32), TOP_K)
    row_tok = jnp.zeros((P,), jnp.int32).at[dest].set(tok_flat)
    x_sorted = _moe_dispatch(x, row_tok)
    y = _moe_ffn(x_sorted, meta, layer, w1, b1, w2, b2)
    return _moe_combine(y, dest, gate)


def _dot_nt(a, b):
    return lax.dot_general(a, b, (((1,), (1,)), ((), ())), preferred_element_type=jnp.float32)


N_SLABS = 2 * NSA_KV
SLAB_ROWS = NSA_BLOCK * N_SLABS
CMP_DENSE_BLOCKS = 64
CMP_PAGES = 64
CMP_UNROLL = 8


def _compress_rows(load_row, n_blocks, pe_ref, w1_ref, w2_ref, out_ref):
    W = 2 * NSA_HD
    for g in range(NSA_KV):
        def step(l, acc):
            x = jnp.concatenate([load_row(l, 2 * g), load_row(l, 2 * g + 1)], axis=1)
            x = (x + pe_ref[pl.ds(l, 1), :]).astype(jnp.bfloat16)
            return acc + jnp.dot(x, w1_ref[l], preferred_element_type=jnp.float32)

        hid = lax.fori_loop(0, NSA_BLOCK, step, jnp.zeros((n_blocks, W), jnp.float32), unroll=CMP_UNROLL)
        hid = hid * (1.0 / (1.0 + jnp.exp(-hid)))
        out_ref[:, g * W:(g + 1) * W] = jnp.dot(hid.astype(jnp.bfloat16), w2_ref[...],
                                                preferred_element_type=jnp.float32)


def _compress_weights(pe, w1, w2):
    z1 = jnp.zeros_like(w1[0])
    w1bd = jnp.concatenate([jnp.concatenate([w1[0], z1], axis=2), jnp.concatenate([z1, w1[1]], axis=2)], axis=1)
    z2 = jnp.zeros_like(w2[0])
    w2bd = jnp.concatenate([jnp.concatenate([w2[0], z2], axis=1), jnp.concatenate([z2, w2[1]], axis=1)], axis=0)
    return pe.reshape(NSA_BLOCK, 2 * NSA_HD), w1bd.astype(jnp.bfloat16), w2bd.astype(jnp.bfloat16)


def _compress_dense_kernel(x_ref, pe_ref, w1_ref, w2_ref, out_ref):
    n_blocks = out_ref.shape[0]
    load = lambda l, j: x_ref[pl.ds(l * N_SLABS + j, n_blocks, stride=SLAB_ROWS), :]
    _compress_rows(load, n_blocks, pe_ref, w1_ref, w2_ref, out_ref)


def _nsa_compress_dense(kv_slab_rows, pe2, w1bd, w2bd):
    nblk = kv_slab_rows.shape[0] // SLAB_ROWS
    step = min(nblk, CMP_DENSE_BLOCKS)
    W4 = N_SLABS * NSA_HD
    const = lambda shape: pl.BlockSpec(shape, lambda i: (0,) * len(shape))
    return pl.pallas_call(
        _compress_dense_kernel,
        grid=(nblk // step,),
        in_specs=[pl.BlockSpec((step * SLAB_ROWS, NSA_HD), lambda i: (i, 0)),
                  const(pe2.shape), const(w1bd.shape), const(w2bd.shape)],
        out_specs=pl.BlockSpec((step, W4), lambda i: (i, 0)),
        out_shape=jax.ShapeDtypeStruct((nblk, W4), jnp.float32),
        compiler_params=pltpu.CompilerParams(dimension_semantics=("arbitrary",),
                                             vmem_limit_bytes=VMEM_LIMIT_BYTES),
        name="nsa_compress_dense",
    )(kv_slab_rows, pe2, w1bd, w2bd)


def _page_copy(pid_ref, cache_ref, buf, sem, layer, step, slot, p):
    rows = cache_ref.shape[2]
    pid = pid_ref[step * CMP_PAGES + p]
    return pltpu.make_async_copy(cache_ref.at[layer, pid], buf.at[slot, pl.ds(p * rows, rows), :], sem.at[slot])


def _compress_paged_kernel(pid_ref, cache_ref, pe_ref, w1_ref, w2_ref, out_ref, buf, sem, *, layer):
    i = pl.program_id(0)
    n_steps = pl.num_programs(0)
    slot = lax.rem(i, 2)

    def for_pages(step, slot_, act):
        def body(p, carry):
            act(_page_copy(pid_ref, cache_ref, buf, sem, layer, step, slot_, p))
            return carry
        lax.fori_loop(0, CMP_PAGES, body, 0)

    @pl.when(i == 0)
    def _():
        for_pages(0, 0, lambda cp: cp.start())

    @pl.when(i + 1 < n_steps)
    def _():
        for_pages(i + 1, 1 - slot, lambda cp: cp.start())

    for_pages(i, slot, lambda cp: cp.wait())
    n_blocks = out_ref.shape[0]
    load = lambda l, j: buf[slot, pl.ds(l * N_SLABS + j, n_blocks, stride=SLAB_ROWS), :]
    _compress_rows(load, n_blocks, pe_ref, w1_ref, w2_ref, out_ref)


def _nsa_compress_paged(cache, layer, page_ids, pe2, w1bd, w2bd):
    page_rows = cache.shape[2]
    n_pages = page_ids.shape[0]
    bpp = page_rows // SLAB_ROWS
    W4 = N_SLABS * NSA_HD
    const = lambda shape: pl.BlockSpec(shape, lambda i, pid: (0,) * len(shape))
    return pl.pallas_call(
        functools.partial(_compress_paged_kernel, layer=layer),
        grid_spec=pltpu.PrefetchScalarGridSpec(
            num_scalar_prefetch=1,
            grid=(n_pages // CMP_PAGES,),
            in_specs=[pl.BlockSpec(memory_space=pl.ANY), const(pe2.shape), const(w1bd.shape), const(w2bd.shape)],
            out_specs=pl.BlockSpec((CMP_PAGES * bpp, W4), lambda i, pid: (i, 0)),
            scratch_shapes=[pltpu.VMEM((2, CMP_PAGES * page_rows, NSA_HD), jnp.float32),
                            pltpu.SemaphoreType.DMA((2,))],
        ),
        out_shape=jax.ShapeDtypeStruct((n_pages * bpp, W4), jnp.float32),
        compiler_params=pltpu.CompilerParams(dimension_semantics=("arbitrary",),
                                             vmem_limit_bytes=VMEM_LIMIT_BYTES),
        name="nsa_compress_paged",
    )(page_ids, cache, pe2, w1bd, w2bd)


NSA_TQ = 128
NSA_TK = 512
LOG2E = 1.4426950408889634
MASK_DIST = 1e30


def _alibi_slope(head_index, n_heads):
    return 2.0 ** (-8.0 * (head_index + 1) / n_heads)


def _nsa_prompt_kernel(q_ref, kc_ref, vc_ref, ks_ref, vs_ref, kw_ref, vw_ref, gl_ref, o_ref,
                       q_s, kc_s, vc_s, ks_s, vs_s, kw_s, vw_s, s_s, p_s, acc_s, m_s, l_s, a_s):
    g = pl.program_id(1)
    qi = pl.program_id(2)
    TQ, HD, R, L = NSA_TQ, NSA_HD, NSA_REP, NSA_BLOCK
    T = ks_ref.shape[1]
    nb = kc_ref.shape[1]
    scale = HD ** -0.5
    bf16 = jnp.bfloat16

    @pl.when(qi == 0)
    def _():
        ks_s[...] = ks_ref[0].astype(bf16)
        vs_s[...] = vs_ref[0].astype(bf16)
        kw_s[...] = kw_ref[0].astype(bf16)
        vw_s[...] = vw_ref[0].astype(bf16)
        kc_s[...] = jnp.zeros_like(kc_s)
        vc_s[...] = jnp.zeros_like(vc_s)
        kc_s[0:nb, :] = kc_ref[0].astype(bf16)
        vc_s[0:nb, :] = vc_ref[0].astype(bf16)

    for r in range(R):
        q_s[r * TQ:(r + 1) * TQ, :] = q_ref[0, :, r * HD:(r + 1) * HD].astype(bf16)

    gate = 1.0 / (1.0 + jnp.exp(-gl_ref[0, 0]))
    slopes = [jnp.where(g == 0, _alibi_slope(r, NSA_HEADS), _alibi_slope(R + r, NSA_HEADS)) for r in range(R)]
    qpos0 = qi * TQ

    row_c = lax.broadcasted_iota(jnp.int32, (TQ, 128), 0) + qpos0
    col_c = lax.broadcasted_iota(jnp.int32, (TQ, 128), 1)
    cur_c = lax.shift_right_logical(row_c, 6)
    mask_c = col_c < cur_c
    dist_c = (row_c - (col_c * L + (L - 1))).astype(jnp.float32)
    imp = jnp.zeros((TQ, 128), jnp.float32)
    s_s[:, 0:128] = _dot_nt(q_s[...], kc_s[...]) * scale
    for r in range(R):
        s = s_s[r * TQ:(r + 1) * TQ, 0:128] - slopes[r] * dist_c
        s = jnp.where(mask_c, s, NEG_INF)
        p = jnp.exp(s - jnp.max(s, axis=-1, keepdims=True))
        p = p / jnp.sum(p, axis=-1, keepdims=True)
        p = jnp.where(mask_c, p, 0.0)
        imp = imp + p
        o = jnp.dot(p.astype(bf16), vc_s[...], preferred_element_type=jnp.float32)
        o_ref[0, :, r * HD:(r + 1) * HD] = gate[:, r:r + 1] * o

    valid = col_c <= cur_c
    forced = valid & ((col_c == 0) | (col_c >= cur_c - 1))
    score = jnp.where(forced, jnp.inf, jnp.where(valid, imp, -jnp.inf))
    sc_t = jnp.transpose(score)[0:32, :]
    jrow = lax.broadcasted_iota(jnp.int32, (32, TQ), 0)
    rank = jnp.zeros((32, TQ), jnp.int32)
    for i in range(32):
        row = sc_t[i:i + 1, :]
        ahead = (row > sc_t) | ((row == sc_t) & (jrow > i))
        rank = rank + ahead.astype(jnp.int32)
    sel_t = jnp.where(rank < NSA_TOPN, 1.0, 0.0)
    sel_t = jnp.concatenate([sel_t, jnp.zeros((128 - 32, TQ), jnp.float32)], axis=0)
    sel = jnp.transpose(sel_t).astype(bf16)

    WK = NSA_WINDOW + TQ
    w0 = pl.multiple_of(jnp.maximum(qi - NSA_WINDOW // TQ, 0) * TQ, TQ)
    row_w = lax.broadcasted_iota(jnp.int32, (TQ, WK), 0) + qpos0
    rel_w = row_w - (lax.broadcasted_iota(jnp.int32, (TQ, WK), 1) + w0)
    ok_w = (rel_w >= 0) & (rel_w < NSA_WINDOW)
    relm_w = jnp.where(ok_w, rel_w.astype(jnp.float32), MASK_DIST)
    s_s[...] = _dot_nt(q_s[...], kw_s[pl.ds(w0, WK), :]) * (scale * LOG2E)
    for r in range(R):
        rows = slice(r * TQ, (r + 1) * TQ)
        s = s_s[rows, :] - (slopes[r] * LOG2E) * relm_w
        p = jnp.exp2(s - jnp.max(s, axis=-1, keepdims=True))
        l_s[rows, :] = jnp.sum(p, axis=-1, keepdims=True)
        p_s[rows, :] = p.astype(bf16)
    acc_s[...] = jnp.dot(p_s[...], vw_s[pl.ds(w0, WK), :], preferred_element_type=jnp.float32) / l_s[...]
    for r in range(R):
        rows = slice(r * TQ, (r + 1) * TQ)
        o_ref[0, :, r * HD:(r + 1) * HD] += gate[:, 2 * R + r:2 * R + r + 1] * acc_s[rows, :]

    m_s[...] = jnp.full_like(m_s, -jnp.inf)
    l_s[...] = jnp.zeros_like(l_s)
    acc_s[...] = jnp.zeros_like(acc_s)
    TK = NSA_TK
    n_chunks = (qpos0 + TQ + TK - 1) // TK

    def chunk(c, carry):
        k0 = pl.multiple_of(c * TK, TK)
        row_t = lax.broadcasted_iota(jnp.int32, (TQ, TK), 0) + qpos0
        rel = row_t - (lax.broadcasted_iota(jnp.int32, (TQ, TK), 1) + k0)
        blk = lax.shift_right_logical(lax.broadcasted_iota(jnp.int32, (128, TK), 1) + k0, 6)
        expand = jnp.where(lax.broadcasted_iota(jnp.int32, (128, TK), 0) == blk, 1.0, 0.0).astype(bf16)
        picked = jnp.dot(sel, expand, preferred_element_type=jnp.float32)
        ok = (picked > 0.5) & (rel >= 0)
        relm = jnp.where(ok, rel.astype(jnp.float32), MASK_DIST)
        s_s[:, 0:TK] = _dot_nt(q_s[...], ks_s[pl.ds(k0, TK), :]) * (scale * LOG2E)
        for r in range(R):
            rows = slice(r * TQ, (r + 1) * TQ)
            s = s_s[rows, 0:TK] - (slopes[r] * LOG2E) * relm
            m_old = m_s[rows, :]
            m_new = jnp.maximum(m_old, jnp.max(s, axis=-1, keepdims=True))
            alpha = jnp.exp2(m_old - m_new)
            p = jnp.exp2(s - m_new)
            l_s[rows, :] = alpha * l_s[rows, :] + jnp.sum(p, axis=-1, keepdims=True)
            p_s[rows, 0:TK] = p.astype(bf16)
            a_s[rows, :] = alpha
            m_s[rows, :] = m_new
        acc_s[...] = a_s[...] * acc_s[...] + jnp.dot(p_s[:, 0:TK], vs_s[pl.ds(k0, TK), :],
                                                     preferred_element_type=jnp.float32)
        return carry

    lax.fori_loop(0, n_chunks, chunk, 0)
    acc_s[...] = acc_s[...] / l_s[...]
    for r in range(R):
        rows = slice(r * TQ, (r + 1) * TQ)
        o_ref[0, :, r * HD:(r + 1) * HD] += gate[:, R + r:R + r + 1] * acc_s[rows, :]


def _nsa_prompt_attention(p, cmp, gate_logits):
    B, T, _ = p.shape
    nb = cmp.shape[1]
    HD, R, G, TQ = NSA_HD, NSA_REP, NSA_KV, NSA_TQ
    kv0 = NSA_HEADS
    lane_kv = lambda branch, c: (lambda b, g, i: (b, 0, kv0 + branch * 2 * G + g * 2 + c))
    bf16 = jnp.bfloat16
    return pl.pallas_call(
        _nsa_prompt_kernel,
        grid=(B, G, T // TQ),
        in_specs=[
            pl.BlockSpec((1, TQ, R * HD), lambda b, g, i: (b, i, g)),
            pl.BlockSpec((1, nb, HD), lambda b, g, i: (b, 0, 2 * g)),
            pl.BlockSpec((1, nb, HD), lambda b, g, i: (b, 0, 2 * g + 1)),
            pl.BlockSpec((1, T, HD), lane_kv(1, 0)),
            pl.BlockSpec((1, T, HD), lane_kv(1, 1)),
            pl.BlockSpec((1, T, HD), lane_kv(2, 0)),
            pl.BlockSpec((1, T, HD), lane_kv(2, 1)),
            pl.BlockSpec((1, 1, TQ, 3 * R), lambda b, g, i: (b, g, i, 0)),
        ],
        out_specs=pl.BlockSpec((1, TQ, R * HD), lambda b, g, i: (b, i, g)),
        out_shape=jax.ShapeDtypeStruct((B, T, NSA_HEADS * HD), jnp.float32),
        scratch_shapes=[
            pltpu.VMEM((R * TQ, HD), bf16),
            pltpu.VMEM((128, HD), bf16), pltpu.VMEM((128, HD), bf16),
            pltpu.VMEM((T, HD), bf16), pltpu.VMEM((T, HD), bf16),
            pltpu.VMEM((T, HD), bf16), pltpu.VMEM((T, HD), bf16),
            pltpu.VMEM((R * TQ, NSA_WINDOW + TQ), jnp.float32),
            pltpu.VMEM((R * TQ, NSA_WINDOW + TQ), bf16),
            pltpu.VMEM((R * TQ, HD), jnp.float32),
            pltpu.VMEM((R * TQ, 1), jnp.float32), pltpu.VMEM((R * TQ, 1), jnp.float32),
            pltpu.VMEM((R * TQ, 1), jnp.float32),
        ],
        compiler_params=pltpu.CompilerParams(dimension_semantics=("arbitrary", "arbitrary", "arbitrary"),
                                             vmem_limit_bytes=VMEM_LIMIT_BYTES),
        name="nsa_prompt_attention",
    )(p, cmp, cmp, p, p, p, p, gate_logits)


NSA_SEL_PAST = NSA_TOPN - 1
SEL_SLOTS = 1024
WIN_SLOTS = NSA_WINDOW + 128
NEW_ROWS = 16


def _row_slopes(g, n_rows):
    r = lax.broadcasted_iota(jnp.int32, (n_rows, 1), 0) % NSA_REP
    return jnp.exp2(-8.0 * (g * NSA_REP + r + 1).astype(jnp.float32) / NSA_HEADS)


def _nsa_sample_cmp_kernel(q_ref, kc_ref, vc_ref, oc_ref, sel_ref):
    g = pl.program_id(1)
    S = q_ref.shape[2] // NSA_REP
    nbp = kc_ref.shape[0]
    L = NSA_BLOCK
    n_rows = S * NSA_REP
    bf16 = jnp.bfloat16
    q = q_ref[0, 0].astype(bf16)
    row = lax.broadcasted_iota(jnp.int32, (n_rows, nbp), 0)
    col = lax.broadcasted_iota(jnp.int32, (n_rows, nbp), 1)
    qpos = PAST_LEN + row // NSA_REP
    cur = qpos // L
    mask = col < cur
    dist = (qpos - (col * L + (L - 1))).astype(jnp.float32)
    s = _dot_nt(q, kc_ref[...].astype(bf16)) * (NSA_HD ** -0.5) - _row_slopes(g, n_rows) * dist
    s = jnp.where(mask, s, NEG_INF)
    p = jnp.exp(s - jnp.max(s, axis=-1, keepdims=True))
    p = p / jnp.sum(p, axis=-1, keepdims=True)
    p = jnp.where(mask, p, 0.0)
    oc_ref[0, 0] = jnp.dot(p.astype(bf16), vc_ref[...].astype(bf16), preferred_element_type=jnp.float32)

    imp = jnp.sum(p.reshape(S, NSA_REP, nbp), axis=1)
    colS = lax.broadcasted_iota(jnp.int32, (S, nbp), 1)
    curS = (PAST_LEN + lax.broadcasted_iota(jnp.int32, (S, nbp), 0)) // L
    valid = colS < curS
    forced = valid & ((colS == 0) | (colS == curS - 1))
    score = jnp.where(forced, jnp.inf, jnp.where(valid, imp, -jnp.inf))
    score_t = jnp.transpose(jnp.concatenate([score, jnp.zeros((128 - S, nbp), jnp.float32)], axis=0))
    ii = lax.broadcasted_iota(jnp.int32, (nbp, nbp), 0)
    jj = lax.broadcasted_iota(jnp.int32, (nbp, nbp), 1)
    lane = lax.broadcasted_iota(jnp.int32, (1, 128), 1)
    jrow = lax.broadcasted_iota(jnp.int32, (1, nbp), 1).astype(jnp.float32)
    out_rows = []
    for si in range(S):
        c = score_t[:, si:si + 1]
        rw = score[si:si + 1, :]
        ahead = (c > rw) | ((c == rw) & (ii < jj))
        rank = jnp.sum(jnp.where(ahead, 1.0, 0.0), axis=0, keepdims=True)
        ids = jnp.zeros((1, 128), jnp.float32)
        for k in range(NSA_SEL_PAST):
            idx = jnp.sum(jnp.where(rank == k, jrow, 0.0), axis=1, keepdims=True)
            ids = jnp.where(lane == k, idx, ids)
        out_rows.append(ids.astype(jnp.int32))
    out_rows.append(jnp.zeros((8 - S, 128), jnp.int32))
    sel_ref[0, 0] = jnp.concatenate(out_rows, axis=0)


def _nsa_sample_cmp(q_rows, cmp_rows):
    Bd, G, n_rows, HD = q_rows.shape
    nbp = cmp_rows.shape[0] // Bd
    return pl.pallas_call(
        _nsa_sample_cmp_kernel,
        grid=(Bd, G),
        in_specs=[pl.BlockSpec((1, 1, n_rows, HD), lambda b, g: (b, g, 0, 0)),
                  pl.BlockSpec((nbp, HD), lambda b, g: (b, 2 * g)),
                  pl.BlockSpec((nbp, HD), lambda b, g: (b, 2 * g + 1))],
        out_specs=[pl.BlockSpec((1, 1, n_rows, HD), lambda b, g: (b, g, 0, 0)),
                   pl.BlockSpec((1, 1, 8, 128), lambda b, g: (b, g, 0, 0))],
        out_shape=[jax.ShapeDtypeStruct((Bd, G, n_rows, HD), jnp.float32),
                   jax.ShapeDtypeStruct((Bd, G, 8, 128), jnp.int32)],
        compiler_params=pltpu.CompilerParams(dimension_semantics=("arbitrary", "arbitrary"),
                                             vmem_limit_bytes=VMEM_LIMIT_BYTES),
        name="nsa_sample_cmp",
    )(q_rows, cmp_rows, cmp_rows)


def _nsa_sample_sel_kernel(phys_ref, sel_ref, q_ref, oc_ref, gate_ref, new_ref, win_ref, *rest):
    blk_refs = rest[:NSA_SEL_PAST]
    o_ref = rest[NSA_SEL_PAST]
    ks_s, vs_s, kw_s, vw_s = rest[NSA_SEL_PAST + 1:]
    b = pl.program_id(0)
    g = pl.program_id(1)
    s_idx = pl.program_id(2)
    S = pl.num_programs(2)
    R, L, HD = NSA_REP, NSA_BLOCK, NSA_HD
    bf16 = jnp.bfloat16
    n_past = NSA_SEL_PAST * L
    wb = win_ref.shape[0] // N_SLABS
    base = ((b * S + s_idx) * NSA_KV + g) * NSA_SEL_PAST
    scale = HD ** -0.5
    slopes = _row_slopes(g, R)
    qpos = PAST_LEN + s_idx

    @pl.when(s_idx == 0)
    def _():
        kw_s[...] = jnp.zeros_like(kw_s)
        vw_s[...] = jnp.zeros_like(vw_s)
        kw_s[0:wb, :] = win_ref[pl.ds(g * 2, wb, stride=N_SLABS), :].astype(bf16)
        vw_s[0:wb, :] = win_ref[pl.ds(g * 2 + 1, wb, stride=N_SLABS), :].astype(bf16)
        kw_s[wb:wb + NEW_ROWS, :] = new_ref[0, 0, 1, 0].astype(bf16)
        vw_s[wb:wb + NEW_ROWS, :] = new_ref[0, 0, 1, 1].astype(bf16)
        ks_s[n_past:, :] = jnp.zeros((SEL_SLOTS - n_past, HD), bf16)
        vs_s[n_past:, :] = jnp.zeros((SEL_SLOTS - n_past, HD), bf16)
        ks_s[n_past:n_past + NEW_ROWS, :] = new_ref[0, 0, 0, 0].astype(bf16)
        vs_s[n_past:n_past + NEW_ROWS, :] = new_ref[0, 0, 0, 1].astype(bf16)

    q = q_ref[0, 0].astype(bf16)
    gate = 1.0 / (1.0 + jnp.exp(-gate_ref[0, 0]))

    lane = lax.broadcasted_iota(jnp.int32, (1, SEL_SLOTS), 1)
    slot = lane // L
    kpos = jnp.where(lane >= n_past, PAST_LEN + (lane - n_past), lane % L)
    for k in range(NSA_SEL_PAST):
        ks_s[k * L:(k + 1) * L, :] = blk_refs[k][pl.ds(g * 2, L, stride=N_SLABS), :].astype(bf16)
        vs_s[k * L:(k + 1) * L, :] = blk_refs[k][pl.ds(g * 2 + 1, L, stride=N_SLABS), :].astype(bf16)
        kpos = kpos + jnp.where(slot == k, sel_ref[base + k] * L, 0)
    rel = qpos - kpos
    ok = (rel >= 0) & (lane < n_past + S)
    sc = _dot_nt(q, ks_s[...]) * scale - slopes * rel.astype(jnp.float32)
    sc = jnp.where(ok, sc, NEG_INF)
    p = jnp.exp(sc - jnp.max(sc, axis=-1, keepdims=True))
    p = p / jnp.sum(p, axis=-1, keepdims=True)
    p = jnp.where(ok, p, 0.0)
    o_s = jnp.dot(p.astype(bf16), vs_s[...], preferred_element_type=jnp.float32)

    lane_w = lax.broadcasted_iota(jnp.int32, (1, WIN_SLOTS), 1)
    kpos_w = jnp.where(lane_w >= wb, PAST_LEN + (lane_w - wb), PAST_LEN - wb + lane_w)
    rel_w = qpos - kpos_w
    ok_w = (rel_w >= 0) & (rel_w < NSA_WINDOW) & (lane_w < wb + S)
    sw = _dot_nt(q, kw_s[...]) * scale - slopes * rel_w.astype(jnp.float32)
    sw = jnp.where(ok_w, sw, NEG_INF)
    pw = jnp.exp(sw - jnp.max(sw, axis=-1, keepdims=True))
    pw = pw / jnp.sum(pw, axis=-1, keepdims=True)
    pw = jnp.where(ok_w, pw, 0.0)
    o_w = jnp.dot(pw.astype(bf16), vw_s[...], preferred_element_type=jnp.float32)

    o_ref[0, 0] = gate[:, 0:1] * oc_ref[0, 0] + gate[:, 1:2] * o_s + gate[:, 2:3] * o_w


def _nsa_sample_sel(phys_blk, sel_blk, q_rows, o_c, gate_rows, new_kv, win_rows, cache_blocks, layer):
    Bd, G, n_rows, HD = q_rows.shape
    S = n_rows // NSA_REP
    R = NSA_REP
    bf16 = jnp.bfloat16
    row_blk = lambda b, g, s, ph, se: (b, g, s, 0)

    def blk_spec(k):
        return pl.BlockSpec((None, None, SLAB_ROWS, HD),
                            lambda b, g, s, ph, se: (layer, ph[((b * S + s) * NSA_KV + g) * NSA_SEL_PAST + k], 0, 0))

    return pl.pallas_call(
        _nsa_sample_sel_kernel,
        grid_spec=pltpu.PrefetchScalarGridSpec(
            num_scalar_prefetch=2,
            grid=(Bd, G, S),
            in_specs=[pl.BlockSpec((1, 1, R, HD), row_blk),
                      pl.BlockSpec((1, 1, R, HD), row_blk),
                      pl.BlockSpec((1, 1, R, 3), row_blk),
                      pl.BlockSpec((1, 1, 2, 2, NEW_ROWS, HD), lambda b, g, s, ph, se: (b, g, 0, 0, 0, 0)),
                      pl.BlockSpec((None, None, win_rows.shape[2], HD), lambda b, g, s, ph, se: (layer, b, 0, 0))]
                     + [blk_spec(k) for k in range(NSA_SEL_PAST)],
            out_specs=pl.BlockSpec((1, 1, R, HD), row_blk),
            scratch_shapes=[pltpu.VMEM((SEL_SLOTS, HD), bf16), pltpu.VMEM((SEL_SLOTS, HD), bf16),
                            pltpu.VMEM((WIN_SLOTS, HD), bf16), pltpu.VMEM((WIN_SLOTS, HD), bf16)],
        ),
        out_shape=jax.ShapeDtypeStruct((Bd, G, n_rows, HD), jnp.float32),
        compiler_params=pltpu.CompilerParams(dimension_semantics=("arbitrary", "arbitrary", "arbitrary"),
                                             vmem_limit_bytes=VMEM_LIMIT_BYTES),
        name="nsa_sample_sel",
    )(phys_blk, sel_blk, q_rows, o_c, gate_rows, new_kv, win_rows, *([cache_blocks] * NSA_SEL_PAST))


GLA_C = 128
GLA_SUB = 16


def _gla_prompt_kernel(q_ref, k_ref, v_ref, g_ref, o_ref, st_ref, st_s, q_s, k_s, b_s, a_s):
    c = pl.program_id(2)
    C, SUB = GLA_C, GLA_SUB
    bf16 = jnp.bfloat16
    f32 = jnp.float32

    @pl.when(c == 0)
    def _():
        st_s[...] = jnp.zeros_like(st_s)

    q = q_ref[0] * (GLA_DK ** -0.5)
    k = k_ref[0]
    g = g_ref[0]
    vb = v_ref[0].astype(bf16)
    tri = jnp.where(lax.broadcasted_iota(jnp.int32, (C, C), 0) >= lax.broadcasted_iota(jnp.int32, (C, C), 1),
                    1.0, 0.0).astype(bf16)
    g_hi = g.astype(bf16)
    r1 = g - g_hi.astype(f32)
    g_mid = r1.astype(bf16)
    g_lo = (r1 - g_mid.astype(f32)).astype(bf16)
    b = (jnp.dot(tri, g_hi, preferred_element_type=f32) + jnp.dot(tri, g_mid, preferred_element_type=f32)
         + jnp.dot(tri, g_lo, preferred_element_type=f32))
    q_s[...] = q
    k_s[...] = k
    b_s[...] = b
    o = _dot_nt((q * jnp.exp(b)).astype(bf16), st_s[...].astype(bf16))

    lane = lax.broadcasted_iota(jnp.int32, (SUB, C), 1)
    trow = lax.broadcasted_iota(jnp.int32, (SUB, 1), 0)

    def sub_block(blk, carry):
        r0 = pl.multiple_of(blk * SUB, SUB)
        qi = q_s[pl.ds(r0, SUB), :]
        bi = b_s[pl.ds(r0, SUB), :]
        ref = jnp.where(blk > 0, b_s[pl.ds(jnp.maximum(r0 - 1, 0), 1), :], 0.0)
        qt = qi * jnp.exp(bi - ref)
        kt = k_s[...] * jnp.exp(jnp.minimum(ref - b_s[...], 0.0))
        a = jnp.where(lane < r0, _dot_nt(qt.astype(bf16), kt.astype(bf16)), 0.0)
        for s in range(SUB):
            ks = k_s[pl.ds(r0 + s, 1), :]
            bs = b_s[pl.ds(r0 + s, 1), :]
            col = jnp.sum(qi * jnp.exp(jnp.minimum(bi - bs, 0.0)) * ks, axis=-1, keepdims=True)
            a = jnp.where(lane == r0 + s, jnp.where(trow >= s, col, 0.0), a)
        a_s[pl.ds(r0, SUB), :] = a
        return carry

    lax.fori_loop(0, C // SUB, sub_block, 0)
    o_ref[0] = o + jnp.dot(a_s[...].astype(bf16), vb, preferred_element_type=f32)

    b_last = b[C - 1:C, :]
    kh = (k * jnp.exp(b_last - b)).astype(bf16)
    upd = lax.dot_general(vb, kh, (((0,), (0,)), ((), ())), preferred_element_type=f32)
    st_s[...] = st_s[...] * jnp.exp(b_last) + upd

    @pl.when(c == pl.num_programs(2) - 1)
    def _():
        st_ref[0, 0] = jnp.transpose(st_s[...])


def _gla_prompt(p, g):
    B, T, _ = p.shape
    H, DK, DV, C = GLA_HEADS, GLA_DK, GLA_DV, GLA_C
    f32 = jnp.float32
    return pl.pallas_call(
        _gla_prompt_kernel,
        grid=(B, H, T // C),
        in_specs=[pl.BlockSpec((1, C, DK), lambda b, h, c: (b, c, h)),
                  pl.BlockSpec((1, C, DK), lambda b, h, c: (b, c, H + h)),
                  pl.BlockSpec((1, C, DV), lambda b, h, c: (b, c, 2 * H * DK // DV + h)),
                  pl.BlockSpec((1, C, DK), lambda b, h, c: (b, c, h))],
        out_specs=[pl.BlockSpec((1, C, DV), lambda b, h, c: (b, c, h)),
                   pl.BlockSpec((1, 1, DK, DV), lambda b, h, c: (b, h, 0, 0))],
        out_shape=[jax.ShapeDtypeStruct((B, T, H * DV), f32), jax.ShapeDtypeStruct((B, H, DK, DV), f32)],
        scratch_shapes=[pltpu.VMEM((DV, DK), f32), pltpu.VMEM((C, DK), f32), pltpu.VMEM((C, DK), f32),
                        pltpu.VMEM((C, DK), f32), pltpu.VMEM((C, C), f32)],
        compiler_params=pltpu.CompilerParams(dimension_semantics=("arbitrary", "arbitrary", "arbitrary"),
                                             vmem_limit_bytes=VMEM_LIMIT_BYTES),
        name="gla_prompt",
    )(p, p, p, g)


def _rmsnorm(x, g):
    xf = x.astype(jnp.float32)
    y = xf * lax.rsqrt(jnp.mean(xf * xf, axis=-1, keepdims=True) + EPS)
    return y.astype(x.dtype) * g


def _alibi_slopes(n_heads, n_groups):
    h = jnp.arange(1, n_heads + 1, dtype=jnp.float32)
    return jnp.exp2(-8.0 * h / n_heads).reshape(n_groups, n_heads // n_groups)


def _adaln(c, w, b):
    m = jax.nn.silu(c) @ w + b
    return m.reshape(c.shape[0], 6, 1, D_MODEL)


def _attend(q, k, v, mask, dist, slopes, sink=None):
    s = jnp.einsum('...qgrd,...kgd->...qgrk', q, k).astype(jnp.float32) * (q.shape[-1] ** -0.5)
    s = s - slopes[:, :, None] * dist[..., :, None, None, :]
    m = mask[..., :, None, None, :]
    s = jnp.where(m, s, NEG_INF)
    if sink is not None:
        sk = jnp.broadcast_to(sink.astype(jnp.float32)[:, :, None], s.shape[:-1] + (1,))
        p = jax.nn.softmax(jnp.concatenate([s, sk], axis=-1), axis=-1)[..., :-1]
    else:
        p = jax.nn.softmax(s, axis=-1)
    p = jnp.where(m, p, 0.0)
    o = jnp.einsum('...qgrk,...kgd->...qgrd', p.astype(v.dtype), v)
    return o, p


def _window_prompt(q, kv, window, slopes, sink=None):
    B, T = q.shape[:2]
    nqb = T // QBLOCK
    n_prev = -(-(window - 1) // QBLOCK)
    kb = (n_prev + 1) * QBLOCK
    pad = jnp.pad(kv, [(0, 0), (n_prev * QBLOCK, 0)] + [(0, 0)] * (kv.ndim - 2))
    blocks = pad.reshape((B, nqb + n_prev, QBLOCK) + kv.shape[2:])
    band = jnp.concatenate([blocks[:, j:j + nqb] for j in range(n_prev + 1)], axis=2)
    qpos = jnp.arange(T).reshape(nqb, QBLOCK)
    kpos = (jnp.arange(nqb)[:, None] - n_prev) * QBLOCK + jnp.arange(kb)[None, :]
    rel = qpos[:, :, None] - kpos[:, None, :]
    mask = (kpos[:, None, :] >= 0) & (rel >= 0) & (rel < window)
    qb = q.reshape((B, nqb, QBLOCK) + q.shape[2:])
    o, _ = _attend(qb, band[..., 0, :], band[..., 1, :], mask, rel.astype(jnp.float32), slopes, sink)
    return o.reshape(q.shape)


def _window_sample(q, kv_new, buf, window, slopes, sink=None):
    S = q.shape[1]
    wb = buf.shape[1]
    kv = jnp.concatenate([buf.astype(kv_new.dtype), kv_new], axis=1)
    qpos = PAST_LEN + jnp.arange(S)
    kpos = PAST_LEN - wb + jnp.arange(wb + S)
    rel = qpos[:, None] - kpos[None, :]
    mask = (rel >= 0) & (rel < window)
    o, _ = _attend(q, kv[..., 0, :], kv[..., 1, :], mask, rel.astype(jnp.float32), slopes, sink)
    return o, kv[:, S:]


def _sel_attend(q, k, v, kpos, qpos, slopes):
    s = jnp.einsum('bqgrd,bqgkd->bqgrk', q, k).astype(jnp.float32) * (q.shape[-1] ** -0.5)
    rel = qpos[None, :, None, None] - kpos
    s = s - slopes[None, None, :, :, None] * rel[:, :, :, None, :].astype(jnp.float32)
    m = (rel >= 0)[:, :, :, None, :]
    p = jnp.where(m, jax.nn.softmax(jnp.where(m, s, NEG_INF), axis=-1), 0.0)
    return jnp.einsum('bqgrk,bqgkd->bqgrd', p.astype(v.dtype), v)


def _nsa_project(h, w_in):
    B, T = h.shape[:2]
    p = h @ w_in
    nq = NSA_HEADS * NSA_HD
    nkv = 2 * NSA_KV * NSA_HD
    q = p[..., :nq].reshape(B, T, NSA_KV, NSA_REP, NSA_HD)
    kv = p[..., nq:nq + 3 * nkv].reshape(B, T, 3, NSA_KV, 2, NSA_HD)
    gates = jax.nn.sigmoid(p[..., nq + 3 * nkv:].reshape(B, T, 3, NSA_KV, NSA_REP))
    return q, kv[:, :, 0], kv[:, :, 1], kv[:, :, 2], gates


def _nsa_compress(blocks, pe, w1, w2):
    x = blocks + pe[:, None]
    hid = jax.nn.silu(jnp.einsum('...nlgcd,cldh->...ngch', x, w1))
    return jnp.einsum('...ngch,chd->...ngcd', hid, w2)


def _nsa_cmp_attend(q, cmp, qpos, slopes):
    nb = cmp.shape[-4]
    j = jnp.arange(nb)
    mask = j[None, :] < (qpos // NSA_BLOCK)[:, None]
    dist = (qpos[:, None] - (j[None, :] * NSA_BLOCK + NSA_BLOCK - 1)).astype(jnp.float32)
    o, p = _attend(q, cmp[..., 0, :], cmp[..., 1, :], mask, dist, slopes)
    return o, p.sum(axis=-2)


def _nsa_merge(gates, o_c, o_s, o_w, w_out):
    o = gates[:, :, 0, :, :, None] * o_c + gates[:, :, 1, :, :, None] * o_s + gates[:, :, 2, :, :, None] * o_w
    return o.reshape(o.shape[0], o.shape[1], -1) @ w_out


def _nsa_layer(hp, hs, cache_cmp, cache_slc, win_all, page_table, li, w_in, pe, w1, w2, w_out):
    L = NSA_BLOCK
    B, T = hp.shape[:2]
    nq = NSA_HEADS * NSA_HD
    nkv = 2 * NSA_KV * NSA_HD
    kv_shape = (NSA_KV, 2, NSA_HD)
    pe2, w1bd, w2bd = _compress_weights(pe, w1, w2)
    pp = hp @ w_in
    kvc_rows = pp[..., nq:nq + nkv]
    kvc = kvc_rows.reshape((B, T) + kv_shape)
    kvs = pp[..., nq + nkv:nq + 2 * nkv].reshape((B, T) + kv_shape)
    win_p = pp[:, T - min(NSA_WINDOW, T):, nq + 2 * nkv:nq + 3 * nkv].reshape((B, min(NSA_WINDOW, T)) + kv_shape)
    cmp_p = _nsa_compress_dense(kvc_rows.reshape(B * T * N_SLABS, NSA_HD), pe2, w1bd, w2bd).reshape(B, T // L, nkv)
    gl = pp[..., nq + 3 * nkv:].reshape(B, T, 3, NSA_KV, NSA_REP).transpose(0, 3, 1, 2, 4)
    o_p = _nsa_prompt_attention(pp, cmp_p, gl.reshape(B, NSA_KV, T, 3 * NSA_REP))
    yp = o_p @ w_out

    Bd, S = hs.shape[:2]
    G, R, HD = NSA_KV, NSA_REP, NSA_HD
    ps = hs @ w_in
    kvc_s = ps[..., nq:nq + nkv].reshape((Bd, S) + kv_shape)
    kvs_s = ps[..., nq + nkv:nq + 2 * nkv].reshape((Bd, S) + kv_shape)
    kvw_s = ps[..., nq + 2 * nkv:nq + 3 * nkv].reshape((Bd, S) + kv_shape)
    q_rows = ps[..., :nq].reshape(Bd, S, G, R, HD).transpose(0, 2, 1, 3, 4).reshape(Bd, G, S * R, HD)
    gate_rows = ps[..., nq + 3 * nkv:].reshape(Bd, S, 3, G, R).transpose(0, 3, 1, 4, 2).reshape(Bd, G, S * R, 3)
    new_kv = ps[..., nq + nkv:nq + 3 * nkv].reshape(Bd, S, 2, G, 2, HD).transpose(0, 3, 2, 4, 1, 5)
    new_kv = jnp.pad(new_kv, [(0, 0)] * 4 + [(0, NEW_ROWS - S), (0, 0)])
    cache_rows = cache_cmp.reshape(cache_cmp.shape[:2] + (PAGE_SIZE * N_SLABS, HD))
    cmp = _nsa_compress_paged(cache_rows, li, page_table.reshape(-1), pe2, w1bd, w2bd)
    o_c, sel_out = _nsa_sample_cmp(q_rows, cmp)
    sel = sel_out[:, :, :S, :NSA_SEL_PAST].transpose(0, 2, 1, 3)
    bpp = PAGE_SIZE // L
    phys = page_table[jnp.arange(Bd)[:, None, None, None], sel // bpp] * bpp + sel % bpp
    cache_blocks = cache_slc.reshape(cache_slc.shape[0], cache_slc.shape[1] * bpp, SLAB_ROWS, HD)
    win_rows = win_all.reshape(win_all.shape[:2] + (win_all.shape[2] * N_SLABS, HD))
    o_s = _nsa_sample_sel(phys.reshape(-1), sel.reshape(-1), q_rows, o_c, gate_rows, new_kv, win_rows,
                          cache_blocks, li)
    ys = o_s.reshape(Bd, G, S, R, HD).transpose(0, 2, 1, 3, 4).reshape(Bd, S, nq) @ w_out
    win_s = jnp.concatenate([win_all[li], kvw_s], axis=1)[:, S:]
    return yp, ys, (kvc, kvc_s, kvs, kvs_s, win_p, win_s)


def _swa_project(h, w_in, b_in):
    B, T = h.shape[:2]
    p = h @ w_in + b_in
    nq = SWA_HEADS * SWA_HD
    q = p[..., :nq].reshape(B, T, SWA_KV, SWA_REP, SWA_HD)
    kv = p[..., nq:].reshape(B, T, SWA_KV, 2, SWA_HD)
    return q, kv


def _swa_layer(hp, hs, buf, w_in, b_in, sinks, w_out, b_out, slopes):
    sink = sinks.reshape(SWA_KV, SWA_REP)
    B, T = hp.shape[:2]
    q, kv = _swa_project(hp, w_in, b_in)
    o = _window_prompt(q, kv, SWA_WINDOW, slopes, sink)
    yp = o.reshape(B, T, -1) @ w_out + b_out
    buf_p = kv[:, T - min(SWA_WINDOW, T):]
    Bd, S = hs.shape[:2]
    q, kv_s = _swa_project(hs, w_in, b_in)
    o, buf_s = _window_sample(q, kv_s, buf, SWA_WINDOW, slopes, sink)
    ys = o.reshape(Bd, S, -1) @ w_out + b_out
    return yp, ys, (buf_p, buf_s)


def _gla_project(h, w_in, w_a2, b_a):
    B, T = h.shape[:2]
    p = h @ w_in
    nk = GLA_HEADS * GLA_DK
    nv = GLA_HEADS * GLA_DV
    q = p[..., :nk].reshape(B, T, GLA_HEADS, GLA_DK).astype(jnp.float32) * (GLA_DK ** -0.5)
    k = p[..., nk:2 * nk].reshape(B, T, GLA_HEADS, GLA_DK).astype(jnp.float32)
    v = p[..., 2 * nk:2 * nk + nv].reshape(B, T, GLA_HEADS, GLA_DV).astype(jnp.float32)
    r = p[..., 2 * nk + nv:2 * nk + 2 * nv].reshape(B, T, GLA_HEADS, GLA_DV)
    a = (p[..., 2 * nk + 2 * nv:] @ w_a2 + b_a).astype(jnp.float32)
    g = (jax.nn.log_sigmoid(a) / GLA_NORMALIZER).reshape(B, T, GLA_HEADS, GLA_DK)
    return q, k, v, g, r


def _gla_chunk(state, q, k, v, g):
    C = q.shape[1]
    b = jnp.cumsum(g, axis=1)
    causal = jnp.tril(jnp.ones((C, C), dtype=bool))
    o_inter = jnp.einsum('bthk,bhkv->bthv', q * jnp.exp(b), state)
    diff = b[:, :, None] - b[:, None, :]
    decay = jnp.exp(jnp.where(causal[None, :, :, None, None], diff, -jnp.inf))
    a = jnp.einsum('bthk,btshk,bshk->btsh', q, decay, k)
    o_intra = jnp.einsum('btsh,bshv->bthv', a, v)
    b_last = b[:, -1]
    new_state = jnp.exp(b_last)[..., None] * state + jnp.einsum('bshk,bshv->bhkv', k * jnp.exp(b_last[:, None] - b), v)
    return new_state, o_inter + o_intra


def _gla_layer(hp, hs, state, w_in, w_a2, b_a, norm, w_out):
    def readout(o, r, h):
        y = _rmsnorm(o, norm).astype(h.dtype) * jax.nn.silu(r)
        return y.reshape(h.shape[0], h.shape[1], -1) @ w_out

    B, T = hp.shape[:2]
    nk = GLA_HEADS * GLA_DK
    nv = GLA_HEADS * GLA_DV
    pp = hp @ w_in
    a = (pp[..., 2 * nk + 2 * nv:] @ w_a2 + b_a).astype(jnp.float32)
    g = jax.nn.log_sigmoid(a) / GLA_NORMALIZER
    o, s_p = _gla_prompt(pp, g)
    r = pp[..., 2 * nk + nv:2 * nk + 2 * nv].reshape(B, T, GLA_HEADS, GLA_DV)
    yp = readout(o.reshape(B, T, GLA_HEADS, GLA_DV), r, hp)
    q, k, v, g, r = _gla_project(hs, w_in, w_a2, b_a)
    s_s, o = _gla_chunk(state.astype(jnp.float32), q, k, v, g)
    ys = readout(o, r, hs)
    return yp, ys, (s_p.astype(hp.dtype), s_s.astype(state.dtype))


def kernel(x_prompt, x_sample, cache_nsa_cmp, cache_nsa_slc, state_nsa_win, state_swa_kv, state_gla,
           page_table, c_prompt, c_sample, ada_w, ada_b, norm_mix, norm_ffn, norm_final,
           nsa_w_in, nsa_cmp_pe, nsa_cmp_w1, nsa_cmp_w2, nsa_w_out,
           swa_w_in, swa_b_in, swa_sinks, swa_w_out, swa_b_out,
           gla_w_in, gla_w_a2, gla_b_a, gla_norm, gla_w_out,
           moe_w_router, moe_b_router, moe_w1, moe_b1, moe_w2, moe_b2):
    B, T = x_prompt.shape[:2]
    Bd, S = x_sample.shape[:2]
    n_p = B * T
    nsa_slopes = _alibi_slopes(NSA_HEADS, NSA_KV)
    swa_slopes = _alibi_slopes(SWA_HEADS, SWA_KV)
    xp, xs = x_prompt, x_sample
    nsa_new, swa_new, gla_new = [], [], []
    for i in range(DEPTH):
        mp = _adaln(c_prompt, ada_w[i], ada_b[i])
        ms = _adaln(c_sample, ada_w[i], ada_b[i])
        hp = _rmsnorm(xp, norm_mix[i]) * (1.0 + mp[:, 1]) + mp[:, 0]
        hs = _rmsnorm(xs, norm_mix[i]) * (1.0 + ms[:, 1]) + ms[:, 0]
        kind, li = i % N_MIXERS, i // N_MIXERS
        if kind == 0:
            yp, ys, st = _nsa_layer(hp, hs, cache_nsa_cmp, cache_nsa_slc, state_nsa_win, page_table, li,
                                    nsa_w_in[li], nsa_cmp_pe[li], nsa_cmp_w1[li], nsa_cmp_w2[li], nsa_w_out[li])
            nsa_new.append(st)
        elif kind == 1:
            yp, ys, st = _swa_layer(hp, hs, state_swa_kv[li], swa_w_in[li], swa_b_in[li], swa_sinks[li],
                                    swa_w_out[li], swa_b_out[li], swa_slopes)
            swa_new.append(st)
        else:
            yp, ys, st = _gla_layer(hp, hs, state_gla[li], gla_w_in[li], gla_w_a2[li], gla_b_a[li],
                                    gla_norm[li], gla_w_out[li])
            gla_new.append(st)
        xp = xp + mp[:, 2] * yp
        xs = xs + ms[:, 2] * ys
        hp = _rmsnorm(xp, norm_ffn[i]) * (1.0 + mp[:, 4]) + mp[:, 3]
        hs = _rmsnorm(xs, norm_ffn[i]) * (1.0 + ms[:, 4]) + ms[:, 3]
        h_all = jnp.concatenate([hp.reshape(n_p, D_MODEL), hs.reshape(Bd * S, D_MODEL)], axis=0)
        y_all = _moe(h_all, moe_w_router[i], moe_b_router[i], i, moe_w1, moe_b1, moe_w2, moe_b2)
        xp = xp + mp[:, 5] * y_all[:n_p].reshape(B, T, D_MODEL)
        xs = xs + ms[:, 5] * y_all[n_p:].reshape(Bd, S, D_MODEL)
    y_prompt = _rmsnorm(xp, norm_final)
    y_sample = _rmsnorm(xs, norm_final)
    stack = lambda sts, k: jnp.stack([st[k] for st in sts])
    return (y_prompt, y_sample, stack(nsa_new, 0), stack(nsa_new, 1), stack(nsa_new, 2), stack(nsa_new, 3),
            stack(nsa_new, 4), stack(nsa_new, 5), stack(swa_new, 0), stack(swa_new, 1),
            stack(gla_new, 0), stack(gla_new, 1))
```
